```python
import functools
import jax
import jax.numpy as jnp
from jax import lax
import numpy as np

D_MODEL = 1024
BATCH = 16
SEQ = 2048
DEPTH = 2
DEC_BATCH = 128
DEC_SEQ = 1
PAST_LEN = 8192
PAGE_SIZE = 128

A_HEADS = 8
A_KV_HEADS = 2
A_HEAD_DIM = 64
A_WIDTH = A_HEADS * A_HEAD_DIM
A_KV_WIDTH = A_KV_HEADS * A_HEAD_DIM
A_SCALE = A_HEAD_DIM ** -0.5
IDX_HEADS = 4
IDX_DIM = 64
TOPK_MAX = 256
B_WIDTH = 512
B_CONV = 3
C_HEADS = 8
C_NOPE = 64
C_ROPE = 32
C_VDIM = 64
C_Q_RANK = 256
C_KV_RANK = 256
C_WIDTH = C_HEADS * C_VDIM
C_SCALE = (C_NOPE + C_ROPE) ** -0.5
ROPE_THETA = 10000.0
D_WIDTH = 512
D_BLOCKS = 8
D_BLOCK = D_WIDTH // D_BLOCKS
D_CONV = 4
LRU_C = 8.0
N_BRANCH = 4
D_FF = ((8 * D_MODEL + 3 * 256 - 1) // (3 * 256)) * 256
Q_BLOCK = 128
EPS = 1e-6
IN_SPLITS = (A_WIDTH, A_KV_WIDTH, A_KV_WIDTH, IDX_HEADS * IDX_DIM, IDX_DIM, IDX_HEADS,
             3 * B_WIDTH, C_Q_RANK, C_KV_RANK, C_ROPE, D_WIDTH, D_WIDTH, N_BRANCH * D_MODEL)
N_IN = sum(IN_SPLITS)

kernel_name = 'hybrid_dsa_conv_mla_rglru_step'


def rmsnorm(x, g):
    xf = x.astype(jnp.float32)
    inv = lax.rsqrt(jnp.mean(xf * xf, axis=-1, keepdims=True) + EPS)
    return (xf * inv).astype(x.dtype) * g


def split_cols(z):
    out, start = [], 0
    for w in IN_SPLITS:
        out.append(z[..., start:start + w])
        start += w
    return out


def rope(x, pos):
    half = x.shape[-1] // 2
    freqs = ROPE_THETA ** (-jnp.arange(half, dtype=jnp.float32) / half)
    ang = pos.astype(jnp.float32)[:, None] * freqs[None, :]
    cos = jnp.cos(ang)[:, None, :]
    sin = jnp.sin(ang)[:, None, :]
    xf = x.astype(jnp.float32)
    x1, x2 = xf[..., :half], xf[..., half:]
    return jnp.concatenate([x1 * cos - x2 * sin, x1 * sin + x2 * cos], axis=-1).astype(x.dtype)


def causal_conv(x, buf, w):
    k = w.shape[0]
    t = x.shape[1]
    xp = jnp.concatenate([buf, x], axis=1)
    y = xp[:, 0:t] * w[0]
    for j in range(1, k):
        y = y + xp[:, j:j + t] * w[j]
    return y, xp[:, -(k - 1):]


def rglru(xc, h0, w_r, b_r, w_i, b_i, lam):
    bn, t = xc.shape[0], xc.shape[1]
    xg = xc.reshape(bn, t, D_BLOCKS, D_BLOCK)
    r = jax.nn.sigmoid(jnp.einsum('btnc,ncd->btnd', xg, w_r).reshape(bn, t, D_WIDTH) + b_r)
    i = jax.nn.sigmoid(jnp.einsum('btnc,ncd->btnd', xg, w_i).reshape(bn, t, D_WIDTH) + b_i)
    log_a = -LRU_C * r.astype(jnp.float32) * jax.nn.softplus(-lam.astype(jnp.float32))
    a = jnp.exp(log_a)
    u = jnp.sqrt(-jnp.expm1(2.0 * log_a)) * (i * xc).astype(jnp.float32)

    def step(hc, au):
        hc = au[0] * hc + au[1]
        return hc, hc

    h_last, hs = lax.scan(step, h0.astype(jnp.float32), (jnp.swapaxes(a, 0, 1), jnp.swapaxes(u, 0, 1)))
    return jnp.swapaxes(hs, 0, 1).astype(xc.dtype), h_last.astype(h0.dtype)


def indexer_scores(qi, wi, ki, allowed):
    rel = jax.nn.relu(jnp.einsum('bthd,bsd->bths', qi, ki))
    score = jnp.einsum('bth,bths->bts', wi, rel).astype(jnp.float32)
    return jnp.where(allowed[None], score, -jnp.inf)


def gathered_attention(q, kg, vg, valid):
    bn, t = q.shape[0], q.shape[1]
    qg = q.reshape(bn, t, A_KV_HEADS, A_HEADS // A_KV_HEADS, A_HEAD_DIM)
    s = jnp.einsum('btgrd,btkgd->btgrk', qg, kg).astype(jnp.float32) * A_SCALE
    s = jnp.where(valid[:, :, None, None, :], s, -jnp.inf)
    p = jax.nn.softmax(s, axis=-1).astype(vg.dtype)
    o = jnp.einsum('btgrk,btkgd->btgrd', p, vg)
    return o.reshape(bn, t, A_WIDTH)


def sparse_attn_prompt(q, k, v, qi, ki, wi):
    bn, s_len = q.shape[0], q.shape[1]
    topk = min(TOPK_MAX, s_len // 4)
    key_pos = jnp.arange(s_len)

    def blk(i):
        start = i * Q_BLOCK
        qb = lax.dynamic_slice_in_dim(q, start, Q_BLOCK, axis=1)
        qib = lax.dynamic_slice_in_dim(qi, start, Q_BLOCK, axis=1)
        wib = lax.dynamic_slice_in_dim(wi, start, Q_BLOCK, axis=1)
        tq = start + jnp.arange(Q_BLOCK)
        score = indexer_scores(qib, wib, ki, key_pos[None, :] <= tq[:, None])
        _, idx = lax.top_k(score, topk)
        kg = jax.vmap(lambda kk, ii: kk[ii])(k, idx)
        vg = jax.vmap(lambda vv, ii: vv[ii])(v, idx)
        return gathered_attention(qb, kg, vg, idx <= tq[None, :, None])

    out = lax.map(blk, jnp.arange(s_len // Q_BLOCK))
    return jnp.swapaxes(out, 0, 1).reshape(bn, s_len, A_WIDTH)


def sparse_attn_sample(q, k_new, v_new, qi, ki_new, wi, layer, cache_k, cache_v, cache_ki, page_table):
    bd, t = q.shape[0], q.shape[1]
    past = page_table.shape[1] * PAGE_SIZE
    total = past + t
    topk = min(TOPK_MAX, total // 4)
    ki_past = cache_ki[layer, page_table].reshape(bd, past, IDX_DIM)
    ki_all = jnp.concatenate([ki_past, ki_new], axis=1)
    pos_q = past + jnp.arange(t)
    score = indexer_scores(qi, wi, ki_all, jnp.arange(total)[None, :] <= pos_q[:, None])
    _, idx = lax.top_k(score, topk)
    in_past = idx < past
    pidx = jnp.minimum(idx, past - 1)
    phys = page_table[jnp.arange(bd)[:, None, None], pidx // PAGE_SIZE]
    off = pidx % PAGE_SIZE
    nidx = jnp.clip(idx - past, 0, t - 1)
    kg = jnp.where(in_past[..., None, None], cache_k[layer, phys, off],
                   jax.vmap(lambda kk, ii: kk[ii])(k_new, nidx))
    vg = jnp.where(in_past[..., None, None], cache_v[layer, phys, off],
                   jax.vmap(lambda vv, ii: vv[ii])(v_new, nidx))
    return gathered_attention(q, kg, vg, idx <= pos_q[None, :, None])


def mla_core(q_lat, q_rope, c_kv, k_rope, allowed):
    s = jnp.einsum('bthr,bsr->bhts', q_lat, c_kv) + jnp.einsum('bthe,bse->bhts', q_rope, k_rope)
    s = jnp.where(allowed[None, None], s.astype(jnp.float32) * C_SCALE, -jnp.inf)
    p = jax.nn.softmax(s, axis=-1).astype(c_kv.dtype)
    return jnp.einsum('bhts,bsr->bthr', p, c_kv)


def mla_attend_prompt(q_lat, q_rope, c_kv, k_rope):
    bn, s_len = q_lat.shape[0], q_lat.shape[1]
    key_pos = jnp.arange(s_len)

    def blk(i):
        start = i * Q_BLOCK
        ql = lax.dynamic_slice_in_dim(q_lat, start, Q_BLOCK, axis=1)
        qr = lax.dynamic_slice_in_dim(q_rope, start, Q_BLOCK, axis=1)
        tq = start + jnp.arange(Q_BLOCK)
        return mla_core(ql, qr, c_kv, k_rope, key_pos[None, :] <= tq[:, None])

    out = lax.map(blk, jnp.arange(s_len // Q_BLOCK))
    return jnp.swapaxes(out, 0, 1).reshape(bn, s_len, C_HEADS, C_KV_RANK)


def mla_attend_sample(q_lat, q_rope, c_kv_new, k_rope_new, layer, cache_c, cache_r, page_table):
    bd, t = q_lat.shape[0], q_lat.shape[1]
    past = page_table.shape[1] * PAGE_SIZE
    c_all = jnp.concatenate([cache_c[layer, page_table].reshape(bd, past, C_KV_RANK), c_kv_new], axis=1)
    r_all = jnp.concatenate([cache_r[layer, page_table].reshape(bd, past, C_ROPE), k_rope_new], axis=1)
    allowed = jnp.arange(past + t)[None, :] <= (past + jnp.arange(t))[:, None]
    return mla_core(q_lat, q_rope, c_all, r_all, allowed)


def mixer_block(h, pos, a_attend, c_attend, b_buf, d_buf, d_h0, lw):
    bn, t = h.shape[0], h.shape[1]
    aq, ak, av, iq, ik, iw, bcx, cq, ckv, ckr, dg, dx, gt = split_cols(h @ lw['w_in'])
    aq = aq.reshape(bn, t, A_HEADS, A_HEAD_DIM)
    ak = ak.reshape(bn, t, A_KV_HEADS, A_HEAD_DIM)
    av = av.reshape(bn, t, A_KV_HEADS, A_HEAD_DIM)
    iq = iq.reshape(bn, t, IDX_HEADS, IDX_DIM)
    ya = a_attend(aq, ak, av, iq, ik, iw)
    gb, gc, xb = jnp.split(bcx, 3, axis=-1)
    yb_conv, b_new = causal_conv(gc * xb, b_buf, lw['conv_b'])
    yb = gb * yb_conv
    q = (rmsnorm(cq, lw['g_cq']) @ lw['w_uq']).reshape(bn, t, C_HEADS, C_NOPE + C_ROPE)
    q_rope = rope(q[..., C_NOPE:], pos)
    q_lat = jnp.einsum('bthn,rhn->bthr', q[..., :C_NOPE], lw['w_uk'])
    c_kv = rmsnorm(ckv, lw['g_ckv'])
    k_rope = rope(ckr[:, :, None, :], pos)[:, :, 0, :]
    o_lat = c_attend(q_lat, q_rope, c_kv, k_rope)
    yc = jnp.einsum('bthr,rhv->bthv', o_lat, lw['w_uv']).reshape(bn, t, C_WIDTH)
    xc, d_new = causal_conv(dx, d_buf, lw['conv_d_w'])
    xc = xc + lw['conv_d_b']
    hs, h_new = rglru(xc, d_h0, lw['w_rg'], lw['b_rg'], lw['w_ig'], lw['b_ig'], lw['lru_lambda'])
    yd = jax.nn.gelu(dg) * hs
    gates = jax.nn.sigmoid(gt.reshape(bn, t, N_BRANCH, D_MODEL))
    merged = (gates[:, :, 0] * (ya @ lw['w_out_a']) + gates[:, :, 1] * (yb @ lw['w_out_b'])
              + gates[:, :, 2] * (yc @ lw['w_out_c']) + gates[:, :, 3] * (yd @ lw['w_out_d']))
    return merged @ lw['w_o'], (ak, av, ik, c_kv, k_rope, b_new, d_new, h_new)


def layer_fwd(x, pos, a_attend, c_attend, b_buf, d_buf, d_h0, lw):
    m, st = mixer_block(rmsnorm(x, lw['g_mix_pre']), pos, a_attend, c_attend, b_buf, d_buf, d_h0, lw)
    x = x + rmsnorm(m, lw['g_mix_post'])
    h = rmsnorm(x, lw['g_ffn_pre'])
    f = (jax.nn.silu(h @ lw['w_ffn_gate']) * (h @ lw['w_ffn_up'])) @ lw['w_ffn_down']
    x = x + rmsnorm(f, lw['g_ffn_post'])
    return x, st


def setup_inputs(seed: int = 0) -> dict:
    key = jax.random.key(seed)
    ks = iter(jax.random.split(key, 48))
    f32 = jnp.float32

    def nrm(shape, scale):
        return jax.random.normal(next(ks), shape, f32) * scale

    def gain(n):
        return 1.0 + nrm((DEPTH, n), 0.05)

    n_pages = PAST_LEN // PAGE_SIZE
    n_pool = (DEC_BATCH * n_pages * 5 + 3) // 4
    inp = {}
    inp['x_prompt'] = nrm((BATCH, SEQ, D_MODEL), 1.0)
    inp['x_sample'] = nrm((DEC_BATCH, DEC_SEQ, D_MODEL), 1.0)
    inp['cache_a_k'] = nrm((DEPTH, n_pool, PAGE_SIZE, A_KV_HEADS, A_HEAD_DIM), 1.0)
    inp['cache_a_v'] = nrm((DEPTH, n_pool, PAGE_SIZE, A_KV_HEADS, A_HEAD_DIM), 1.0)
    inp['cache_a_kidx'] = nrm((DEPTH, n_pool, PAGE_SIZE, IDX_DIM), 1.0)
    inp['cache_c_kv'] = nrm((DEPTH, n_pool, PAGE_SIZE, C_KV_RANK), 1.0)
    inp['cache_c_krope'] = nrm((DEPTH, n_pool, PAGE_SIZE, C_ROPE), 1.0)
    inp['state_b_conv'] = nrm((DEPTH, DEC_BATCH, B_CONV - 1, B_WIDTH), 1.0)
    inp['state_d_conv'] = nrm((DEPTH, DEC_BATCH, D_CONV - 1, D_WIDTH), 1.0)
    inp['state_d_h'] = nrm((DEPTH, DEC_BATCH, D_WIDTH), 1.0)
    perm = jax.random.permutation(next(ks), n_pool)[:DEC_BATCH * n_pages]
    inp['page_table'] = perm.reshape(DEC_BATCH, n_pages).astype(jnp.int32)
    inp['g_mix_pre'] = gain(D_MODEL)
    inp['g_mix_post'] = gain(D_MODEL)
    inp['g_ffn_pre'] = gain(D_MODEL)
    inp['g_ffn_post'] = gain(D_MODEL)
    inp['w_in'] = nrm((DEPTH, D_MODEL, N_IN), D_MODEL ** -0.5)
    inp['w_out_a'] = nrm((DEPTH, A_WIDTH, D_MODEL), A_WIDTH ** -0.5)
    inp['conv_b'] = nrm((DEPTH, B_CONV, B_WIDTH), B_CONV ** -0.5)
    inp['w_out_b'] = nrm((DEPTH, B_WIDTH, D_MODEL), B_WIDTH ** -0.5)
    inp['g_cq'] = gain(C_Q_RANK)
    inp['w_uq'] = nrm((DEPTH, C_Q_RANK, C_HEADS * (C_NOPE + C_ROPE)), C_Q_RANK ** -0.5)
    inp['g_ckv'] = gain(C_KV_RANK)
    inp['w_uk'] = nrm((DEPTH, C_KV_RANK, C_HEADS, C_NOPE), C_KV_RANK ** -0.5)
    inp['w_uv'] = nrm((DEPTH, C_KV_RANK, C_HEADS, C_VDIM), C_KV_RANK ** -0.5)
    inp['w_out_c'] = nrm((DEPTH, C_WIDTH, D_MODEL), C_WIDTH ** -0.5)
    inp['conv_d_w'] = nrm((DEPTH, D_CONV, D_WIDTH), D_CONV ** -0.5)
    inp['conv_d_b'] = nrm((DEPTH, D_WIDTH), 0.02)
    inp['w_rg'] = nrm((DEPTH, D_BLOCKS, D_BLOCK, D_BLOCK), D_BLOCK ** -0.5)
    inp['b_rg'] = nrm((DEPTH, D_WIDTH), 0.1)
    inp['w_ig'] = nrm((DEPTH, D_BLOCKS, D_BLOCK, D_BLOCK), D_BLOCK ** -0.5)
    inp['b_ig'] = nrm((DEPTH, D_WIDTH), 0.1)
    u = jax.random.uniform(next(ks), (DEPTH, D_WIDTH), f32, 0.9, 0.999)
    a0 = u ** (1.0 / LRU_C)
    inp['lru_lambda'] = jnp.log(a0) - jnp.log1p(-a0)
    inp['w_out_d'] = nrm((DEPTH, D_WIDTH, D_MODEL), D_WIDTH ** -0.5)
    inp['w_o'] = nrm((DEPTH, D_MODEL, D_MODEL), D_MODEL ** -0.5)
    inp['w_ffn_gate'] = nrm((DEPTH, D_MODEL, D_FF), D_MODEL ** -0.5)
    inp['w_ffn_up'] = nrm((DEPTH, D_MODEL, D_FF), D_MODEL ** -0.5)
    inp['w_ffn_down'] = nrm((DEPTH, D_FF, D_MODEL), D_FF ** -0.5)
    return inp


def reference(x_prompt, x_sample, cache_a_k, cache_a_v, cache_a_kidx, cache_c_kv, cache_c_krope,
              state_b_conv, state_d_conv, state_d_h, page_table,
              g_mix_pre, g_mix_post, g_ffn_pre, g_ffn_post, w_in, w_out_a, conv_b, w_out_b,
              g_cq, w_uq, g_ckv, w_uk, w_uv, w_out_c, conv_d_w, conv_d_b, w_rg, b_rg, w_ig, b_ig,
              lru_lambda, w_out_d, w_o, w_ffn_gate, w_ffn_up, w_ffn_down):
    bp, s_len = x_prompt.shape[0], x_prompt.shape[1]
    t_len = x_sample.shape[1]
    past = page_table.shape[1] * PAGE_SIZE
    pos_p = jnp.arange(s_len)
    pos_s = past + jnp.arange(t_len)
    xp, xs = x_prompt, x_sample
    p_states, s_states = [], []
    for l in range(DEPTH):
        lw = {'g_mix_pre': g_mix_pre[l], 'g_mix_post': g_mix_post[l], 'g_ffn_pre': g_ffn_pre[l],
              'g_ffn_post': g_ffn_post[l], 'w_in': w_in[l], 'w_out_a': w_out_a[l], 'conv_b': conv_b[l],
              'w_out_b': w_out_b[l], 'g_cq': g_cq[l], 'w_uq': w_uq[l], 'g_ckv': g_ckv[l], 'w_uk': w_uk[l],
              'w_uv': w_uv[l], 'w_out_c': w_out_c[l], 'conv_d_w': conv_d_w[l], 'conv_d_b': conv_d_b[l],
              'w_rg': w_rg[l], 'b_rg': b_rg[l], 'w_ig': w_ig[l], 'b_ig': b_ig[l],
              'lru_lambda': lru_lambda[l], 'w_out_d': w_out_d[l], 'w_o': w_o[l],
              'w_ffn_gate': w_ffn_gate[l], 'w_ffn_up': w_ffn_up[l], 'w_ffn_down': w_ffn_down[l]}
        xp, st_p = layer_fwd(xp, pos_p, sparse_attn_prompt, mla_attend_prompt,
                             jnp.zeros((bp, B_CONV - 1, B_WIDTH), xp.dtype),
                             jnp.zeros((bp, D_CONV - 1, D_WIDTH), xp.dtype),
                             jnp.zeros((bp, D_WIDTH), xp.dtype), lw)
        a_s = functools.partial(sparse_attn_sample, layer=l, cache_k=cache_a_k, cache_v=cache_a_v,
                                cache_ki=cache_a_kidx, page_table=page_table)
        c_s = functools.partial(mla_attend_sample, layer=l, cache_c=cache_c_kv, cache_r=cache_c_krope,
                                page_table=page_table)
        xs, st_s = layer_fwd(xs, pos_s, a_s, c_s, state_b_conv[l], state_d_conv[l], state_d_h[l], lw)
        p_states.append(st_p)
        s_states.append(st_s)
    p_a_k, p_a_v, p_a_kidx, p_c_kv, p_c_krope, p_b_conv, p_d_conv, p_d_h = [
        jnp.stack([st[j] for st in p_states]) for j in range(8)]
    s_a_k, s_a_v, s_a_kidx, s_c_kv, s_c_krope, s_b_conv, s_d_conv, s_d_h = [
        jnp.stack([st[j] for st in s_states]) for j in range(8)]
    return (xp, xs, p_a_k, p_a_v, p_a_kidx, p_c_kv, p_c_krope, p_b_conv, p_d_conv, p_d_h,
            s_a_k, s_a_v, s_a_kidx, s_c_kv, s_c_krope, s_b_conv, s_d_conv, s_d_h)
```

```python
import functools

import numpy as np
import jax
import jax.numpy as jnp
from jax import lax
from jax.experimental import pallas as pl
from jax.experimental.pallas import tpu as pltpu

D_MODEL = 1024
PAGE_SIZE = 128
A_HEADS = 8
A_KV_HEADS = 2
A_HEAD_DIM = 64
A_WIDTH = A_HEADS * A_HEAD_DIM
A_SCALE = A_HEAD_DIM ** -0.5
IDX_HEADS = 4
IDX_DIM = 64
TOPK_MAX = 256
B_WIDTH = 512
B_CONV = 3
C_HEADS = 8
C_NOPE = 64
C_ROPE = 32
C_VDIM = 64
C_Q_RANK = 256
C_KV_RANK = 256
C_WIDTH = C_HEADS * C_VDIM
C_SCALE = (C_NOPE + C_ROPE) ** -0.5
ROPE_THETA = 10000.0
D_WIDTH = 512
D_BLOCKS = 8
D_BLOCK = D_WIDTH // D_BLOCKS
D_CONV = 4
LRU_C = 8.0
N_BRANCH = 4
D_FF = ((8 * D_MODEL + 3 * 256 - 1) // (3 * 256)) * 256
Q_BLOCK = 128
EPS = 1e-6

LANES = 128
VMEM_LIMIT = 56 * 1024 * 1024
NEG_BIG = -1e30

BF16 = jnp.bfloat16
F32 = jnp.float32

_SRC = {}
_o = 0
for _n, _w in (("aq", A_WIDTH), ("ak", 128), ("av", 128), ("iq", 256), ("ik", 64), ("iw", 4),
               ("gb", 512), ("gc", 512), ("xb", 512), ("cq", 256), ("ckv", 256), ("ckr", 32),
               ("dg", 512), ("dx", 512), ("gt", 4096)):
    _SRC[_n] = (_o, _w)
    _o += _w
N_IN = _o

_DST = {}
_o = 0
for _n, _w in (("aq", 512), ("gb", 512), ("gc", 512), ("xb", 512), ("dg", 512), ("dx", 512),
               ("gt", 4096), ("iq", 256), ("cq", 256), ("ckv", 256), ("ckr", 256),
               ("ak", 256), ("av", 256), ("ik", 128), ("iw", 128)):
    _DST[_n] = (_o, _w)
    _o += _w
N_PROJ = _o
PROJ_TN = 1280


def _blk(name, width):
    off = _DST[name][0]
    assert off % width == 0
    return off // width


def _cparams(sem, vmem=VMEM_LIMIT):
    return pltpu.CompilerParams(dimension_semantics=sem, vmem_limit_bytes=vmem)


def _relayout_w_in(w):
    def src(name, lo=0, hi=None):
        o, wd = _SRC[name]
        hi = wd if hi is None else hi
        return w[:, o + lo:o + hi]

    zeros = lambda n: jnp.zeros((w.shape[0], n), w.dtype)
    parts = [src("aq"), src("gb"), src("gc"), src("xb"), src("dg"), src("dx"), src("gt"),
             src("iq"), src("cq"), src("ckv"),
             jnp.tile(src("ckr", 0, 16), (1, 8)), jnp.tile(src("ckr", 16, 32), (1, 8)),
             src("ak", 0, 64), src("ak", 0, 64), src("ak", 64, 128), src("ak", 64, 128),
             src("av", 0, 64), src("av", 0, 64), src("av", 64, 128), src("av", 64, 128),
             src("ik"), src("ik"),
             src("iw"), zeros(124)]
    out = jnp.concatenate(parts, axis=1)
    assert out.shape[1] == N_PROJ
    return out.astype(BF16)


def _rms(x, g):
    inv = lax.rsqrt(jnp.mean(x * x, axis=-1, keepdims=True) + EPS)
    return (x * inv) * g


def _dot(a, b):
    return jnp.dot(a, b, preferred_element_type=F32)


def _dot_nt(a, b):
    return lax.dot_general(a, b, (((1,), (1,)), ((), ())), preferred_element_type=F32)


def _inproj_kernel(x_ref, g_ref, w_ref, o_ref, h_ref):
    @pl.when(pl.program_id(1) == 0)
    def _():
        h_ref[...] = _rms(x_ref[...], g_ref[...]).astype(BF16)

    o_ref[...] = _dot(h_ref[...], w_ref[...])


def _inproj(x, g, w, tm):
    m = x.shape[0]
    tm = min(tm, m)
    return pl.pallas_call(
        _inproj_kernel,
        grid=(m // tm, N_PROJ // PROJ_TN),
        in_specs=[pl.BlockSpec((tm, D_MODEL), lambda i, j: (i, 0)),
                  pl.BlockSpec((1, D_MODEL), lambda i, j: (0, 0)),
                  pl.BlockSpec((D_MODEL, PROJ_TN), lambda i, j: (0, j))],
        out_specs=pl.BlockSpec((tm, PROJ_TN), lambda i, j: (i, j)),
        out_shape=jax.ShapeDtypeStruct((m, N_PROJ), F32),
        scratch_shapes=[pltpu.VMEM((tm, D_MODEL), BF16)],
        compiler_params=_cparams(("parallel", "arbitrary")),
        name="in_proj",
    )(x, g, w)


def _cprep_kernel(cq_ref, ckv_ref, ckr_ref, cos_ref, sin_ref, gq_ref, gkv_ref, wuq_ref, wuk_ref,
                  q_ref, kcat_ref, ckvn_ref, kr_ref, qr_ref):
    cos = cos_ref[...]
    sin = sin_ref[...]
    cqn = _rms(cq_ref[...], gq_ref[...]).astype(BF16)
    q = _dot(cqn, wuq_ref[...])
    nope = C_HEADS * C_NOPE
    r1 = q[:, nope:nope + LANES]
    r2 = q[:, nope + LANES:nope + 2 * LANES]
    o1 = r1 * cos - r2 * sin
    o2 = r1 * sin + r2 * cos
    qr_ref[...] = jnp.concatenate([o1, o2], axis=-1)
    head_of_lane = lax.broadcasted_iota(jnp.int32, o1.shape, 1) // (C_ROPE // 2)
    for p in range(C_HEADS // 2):
        ql = _dot(q[:, LANES * p:LANES * (p + 1)].astype(BF16), wuk_ref[p])
        for e in range(2):
            h = 2 * p + e
            sel = head_of_lane == h
            q_ref[h] = jnp.concatenate(
                [ql[:, C_KV_RANK * e:C_KV_RANK * (e + 1)],
                 jnp.where(sel, o1, 0.0), jnp.where(sel, o2, 0.0)], axis=-1).astype(BF16)
    ckvn = _rms(ckv_ref[...], gkv_ref[...])
    ckvn_ref[...] = ckvn
    c = ckr_ref[...]
    c1 = c[:, :LANES]
    c2 = c[:, LANES:]
    k1 = c1 * cos - c2 * sin
    k2 = c1 * sin + c2 * cos
    kr_ref[...] = jnp.concatenate([k1, k2], axis=-1)
    kcat_ref[...] = jnp.concatenate([ckvn, k1, k2], axis=-1).astype(BF16)


def _cprep(z, cos, sin, tab_map, g_cq, g_ckv, wuq, wukp, tm):
    m = z.shape[0]
    tm = min(tm, m)
    qw = C_KV_RANK + 2 * LANES
    const2 = lambda i: (0, 0)
    return pl.pallas_call(
        _cprep_kernel,
        grid=(m // tm,),
        in_specs=[pl.BlockSpec((tm, 256), lambda i: (i, _blk("cq", 256))),
                  pl.BlockSpec((tm, 256), lambda i: (i, _blk("ckv", 256))),
                  pl.BlockSpec((tm, 256), lambda i: (i, _blk("ckr", 256))),
                  pl.BlockSpec((tm, LANES), tab_map),
                  pl.BlockSpec((tm, LANES), tab_map),
                  pl.BlockSpec((1, C_Q_RANK), const2),
                  pl.BlockSpec((1, C_KV_RANK), const2),
                  pl.BlockSpec(wuq.shape, const2),
                  pl.BlockSpec(wukp.shape, lambda i: (0, 0, 0))],
        out_specs=[pl.BlockSpec((C_HEADS, tm, qw), lambda i: (0, i, 0)),
                   pl.BlockSpec((tm, qw), lambda i: (i, 0)),
                   pl.BlockSpec((tm, C_KV_RANK), lambda i: (i, 0)),
                   pl.BlockSpec((tm, 2 * LANES), lambda i: (i, 0)),
                   pl.BlockSpec((tm, 2 * LANES), lambda i: (i, 0))],
        out_shape=[jax.ShapeDtypeStruct((C_HEADS, m, qw), BF16),
                   jax.ShapeDtypeStruct((m, qw), BF16),
                   jax.ShapeDtypeStruct((m, C_KV_RANK), F32),
                   jax.ShapeDtypeStruct((m, 2 * LANES), F32),
                   jax.ShapeDtypeStruct((m, 2 * LANES), F32)],
        compiler_params=_cparams(("parallel",)),
        name="c_prep",
    )(z, z, z, cos, sin, g_cq, g_ckv, wuq, wukp)


def _mla_kernel(q_ref, k_ref, wuv_ref, o_ref, m_ref, l_ref, acc_ref, *, tk, nkb):
    i = pl.program_id(1)
    j = pl.program_id(2)
    rows = C_HEADS * Q_BLOCK

    @pl.when(j == 0)
    def _():
        m_ref[...] = jnp.full(m_ref.shape, NEG_BIG, F32)
        l_ref[...] = jnp.zeros(l_ref.shape, F32)
        acc_ref[...] = jnp.zeros(acc_ref.shape, F32)

    @pl.when(j * tk <= i * Q_BLOCK + Q_BLOCK - 1)
    def _():
        q = q_ref[...].reshape(rows, q_ref.shape[-1])
        k = k_ref[...]
        s = _dot_nt(q, k) * C_SCALE
        t = i * Q_BLOCK + lax.broadcasted_iota(jnp.int32, s.shape, 0) % Q_BLOCK
        col = j * tk + lax.broadcasted_iota(jnp.int32, s.shape, 1)
        s = jnp.where(col <= t, s, -jnp.inf)
        m_old = m_ref[...]
        m_new = jnp.maximum(m_old, jnp.max(s, axis=1, keepdims=True))
        alpha = jnp.exp(m_old - m_new)
        p = jnp.exp(s - m_new)
        l_ref[...] = alpha * l_ref[...] + jnp.sum(p, axis=1, keepdims=True)
        acc_ref[...] = alpha * acc_ref[...] + _dot(p.astype(BF16), k[:, :C_KV_RANK])
        m_ref[...] = m_new

    @pl.when(j == nkb - 1)
    def _():
        o = acc_ref[...] / l_ref[...]
        outs = []
        for p in range(C_HEADS // 2):
            pair = jnp.concatenate([o[2 * p * Q_BLOCK:(2 * p + 1) * Q_BLOCK],
                                    o[(2 * p + 1) * Q_BLOCK:(2 * p + 2) * Q_BLOCK]], axis=-1)
            outs.append(_dot(pair.astype(BF16), wuv_ref[p]))
        o_ref[...] = jnp.concatenate(outs, axis=-1).astype(o_ref.dtype)


def _mla_prompt(q, kcat, wuvp, bn, s_len):
    nq = s_len // Q_BLOCK
    tk = min(256, s_len)
    nkb = s_len // tk
    qw = q.shape[-1]

    def k_map(b, i, j):
        last = (i * Q_BLOCK + Q_BLOCK - 1) // tk
        return (b * nkb + jnp.minimum(j, last), 0)

    return pl.pallas_call(
        functools.partial(_mla_kernel, tk=tk, nkb=nkb),
        grid=(bn, nq, nkb),
        in_specs=[pl.BlockSpec((C_HEADS, Q_BLOCK, qw), lambda b, i, j: (0, b * nq + i, 0)),
                  pl.BlockSpec((tk, qw), k_map),
                  pl.BlockSpec(wuvp.shape, lambda b, i, j: (0, 0, 0))],
        out_specs=pl.BlockSpec((Q_BLOCK, C_WIDTH), lambda b, i, j: (b * nq + i, 0)),
        out_shape=jax.ShapeDtypeStruct((bn * s_len, C_WIDTH), BF16),
        scratch_shapes=[pltpu.VMEM((C_HEADS * Q_BLOCK, 1), F32),
                        pltpu.VMEM((C_HEADS * Q_BLOCK, 1), F32),
                        pltpu.VMEM((C_HEADS * Q_BLOCK, C_KV_RANK), F32)],
        compiler_params=_cparams(("parallel", "parallel", "arbitrary")),
        name="mla_prompt",
    )(q, kcat, wuvp)


def _rowsum(x):
    return jnp.sum(x, axis=1, keepdims=True)


def _topk_select(xs, allowed, col, k, nbits):
    kf = float(k)
    inf = jnp.float32(jnp.inf)
    lo0 = jnp.min(jnp.where(allowed, xs, inf), axis=1, keepdims=True)
    ub0 = jnp.max(xs, axis=1, keepdims=True)
    n_allowed = _rowsum(jnp.where(allowed, 1.0, 0.0))
    ub0 = jnp.where(n_allowed > kf, ub0, lo0)

    def cond(c):
        lo, ub = c
        return jnp.max(jnp.where(lo < ub, 1.0, 0.0)) > 0.0

    def body(c):
        lo, ub = c
        mid = lo + (ub - lo) * 0.5
        mid = jnp.where((mid >= lo) & (mid < ub), mid, lo)
        gt = xs > mid
        cnt = _rowsum(jnp.where(gt, 1.0, 0.0))
        vmin = jnp.min(jnp.where(gt, xs, inf), axis=1, keepdims=True)
        vmax = jnp.max(jnp.where(gt, -inf, xs), axis=1, keepdims=True)
        active = lo < ub
        up = cnt >= kf
        lo = jnp.where(active & up, vmin, lo)
        ub = jnp.where(active & jnp.logical_not(up), vmax, ub)
        return lo, ub

    v, _ = lax.while_loop(cond, body, (lo0, ub0))
    gtv = xs > v
    need = kf - _rowsum(jnp.where(gtv, 1.0, 0.0))
    tie = allowed & (xs == v)
    n_tie = _rowsum(jnp.where(tie, 1.0, 0.0))

    def search(_):
        def step(it, c):
            cand = c + jnp.left_shift(jnp.int32(1), nbits - 1 - it)
            f = _rowsum(jnp.where(tie & (col < cand), 1.0, 0.0))
            return jnp.where(f < need, cand, c)
        return lax.fori_loop(0, nbits, step, jnp.zeros(v.shape, jnp.int32))

    def no_search(_):
        return jnp.full(v.shape, (1 << nbits) - 1, jnp.int32)

    any_over = jnp.max(jnp.where(n_tie > need, 1.0, 0.0)) > 0.0
    cut = lax.cond(any_over, search, no_search, 0)
    return gtv | (tie & (col <= cut))


def _sparse_prompt_kernel(q_ref, iq_ref, iw_ref, kd_ref, vd_ref, ikd_ref, o_ref, *, topk, nbits):
    i = pl.program_id(1)
    q = q_ref[...]
    iq = iq_ref[...]
    iw = iw_ref[...]
    ikd = ikd_ref[...].astype(BF16)
    s_len = ikd.shape[0]
    first_half = lax.broadcasted_iota(jnp.int32, (Q_BLOCK, LANES), 1) < (LANES // 2)

    def half(x, pair, e):
        blk = x[:, LANES * pair:LANES * (pair + 1)]
        return jnp.where(first_half if e == 0 else jnp.logical_not(first_half), blk, 0.0).astype(BF16)

    score = None
    for h in range(IDX_HEADS):
        rel = jnp.maximum(_dot_nt(half(iq, h // 2, h % 2), ikd), 0.0)
        term = iw[:, h:h + 1] * rel
        score = term if score is None else score + term
    t = i * Q_BLOCK + lax.broadcasted_iota(jnp.int32, score.shape, 0)
    col = lax.broadcasted_iota(jnp.int32, score.shape, 1)
    allowed = col <= t
    xs = jnp.where(allowed, score + 0.0, -jnp.inf)
    sel = _topk_select(xs, allowed, col, topk, nbits)

    rep = A_HEADS // A_KV_HEADS
    outs = []
    for g in range(A_KV_HEADS):
        kd = kd_ref[:, LANES * g:LANES * (g + 1)].astype(BF16)
        vd = vd_ref[:, LANES * g:LANES * (g + 1)].astype(BF16)
        for r in range(rep):
            h = g * rep + r
            s = _dot_nt(half(q, h // 2, h % 2), kd) * A_SCALE
            s = jnp.where(sel, s, -jnp.inf)
            m = jnp.max(s, axis=1, keepdims=True)
            p = jnp.exp(s - m)
            l = _rowsum(p)
            outs.append(_dot(p.astype(BF16), vd) / l)
    pairs = [jnp.where(first_half, outs[2 * j], outs[2 * j + 1]) for j in range(A_HEADS // 2)]
    o_ref[...] = jnp.concatenate(pairs, axis=-1).astype(o_ref.dtype)


def _sparse_prompt(z3, topk):
    bn, s_len, _ = z3.shape
    nq = s_len // Q_BLOCK
    nbits = max(1, (s_len - 1).bit_length())
    return pl.pallas_call(
        functools.partial(_sparse_prompt_kernel, topk=topk, nbits=nbits),
        grid=(bn, nq),
        in_specs=[pl.BlockSpec((None, Q_BLOCK, 512), lambda b, i: (b, i, _blk("aq", 512))),
                  pl.BlockSpec((None, Q_BLOCK, 256), lambda b, i: (b, i, _blk("iq", 256))),
                  pl.BlockSpec((None, Q_BLOCK, 128), lambda b, i: (b, i, _blk("iw", 128))),
                  pl.BlockSpec((None, s_len, 256), lambda b, i: (b, 0, _blk("ak", 256))),
                  pl.BlockSpec((None, s_len, 256), lambda b, i: (b, 0, _blk("av", 256))),
                  pl.BlockSpec((None, s_len, 128), lambda b, i: (b, 0, _blk("ik", 128)))],
        out_specs=pl.BlockSpec((None, Q_BLOCK, A_WIDTH), lambda b, i: (b, i, 0)),
        out_shape=jax.ShapeDtypeStruct((bn, s_len, A_WIDTH), BF16),
        compiler_params=_cparams(("parallel", "parallel")),
        name="sparse_prompt",
    )(z3, z3, z3, z3, z3, z3)


PAD_ROWS = 8


def _bconv_kernel(gb_ref, gc_ref, xb_ref, w_ref, y_ref, tail_ref, pad_ref):
    tb = gb_ref.shape[0]

    @pl.when(pl.program_id(1) == 0)
    def _():
        pad_ref[0:PAD_ROWS, :] = jnp.zeros((PAD_ROWS, B_WIDTH), F32)

    p = gc_ref[...] * xb_ref[...]
    pad_ref[PAD_ROWS:PAD_ROWS + tb, :] = p
    w = w_ref[...]
    y = (pad_ref[PAD_ROWS - 2:PAD_ROWS - 2 + tb, :] * w[0:1]
         + pad_ref[PAD_ROWS - 1:PAD_ROWS - 1 + tb, :] * w[1:2]
         + p * w[2:3])
    y_ref[...] = (gb_ref[...] * y).astype(y_ref.dtype)
    tail_ref[...] = pad_ref[PAD_ROWS + tb - (B_CONV - 1):PAD_ROWS + tb, :]
    pad_ref[0:PAD_ROWS, :] = pad_ref[tb:tb + PAD_ROWS, :]


def _bconv_prompt(z3, conv_w):
    bn, s_len, _ = z3.shape
    tb = min(256, s_len)
    nt = s_len // tb
    spec = lambda name: pl.BlockSpec((None, tb, 512), lambda b, t: (b, t, _blk(name, 512)))
    return pl.pallas_call(
        _bconv_kernel,
        grid=(bn, nt),
        in_specs=[spec("gb"), spec("gc"), spec("xb"),
                  pl.BlockSpec((B_CONV, B_WIDTH), lambda b, t: (0, 0))],
        out_specs=[pl.BlockSpec((None, tb, B_WIDTH), lambda b, t: (b, t, 0)),
                   pl.BlockSpec((None, B_CONV - 1, B_WIDTH), lambda b, t: (b, 0, 0))],
        out_shape=[jax.ShapeDtypeStruct((bn, s_len, B_WIDTH), BF16),
                   jax.ShapeDtypeStruct((bn, B_CONV - 1, B_WIDTH), F32)],
        scratch_shapes=[pltpu.VMEM((PAD_ROWS + tb, B_WIDTH), F32)],
        compiler_params=_cparams(("parallel", "arbitrary")),
        name="bconv_prompt",
    )(z3, z3, z3, conv_w)


def _lru_coeffs(xc, wr, br, wi, bi, lam):
    xcb = xc.astype(BF16)
    r = jax.nn.sigmoid(_dot(xcb, wr) + br)
    ig = jax.nn.sigmoid(_dot(xcb, wi) + bi)
    nl = -lam
    softplus = jnp.maximum(nl, 0.0) + jnp.log1p(jnp.exp(-jnp.abs(nl)))
    log_a = -LRU_C * r * softplus
    a = jnp.exp(log_a)
    th = jnp.tanh(log_a)
    u = jnp.sqrt(-2.0 * th / (1.0 - th)) * (ig * xc)
    return a, u


def _rglru_kernel(dx_ref, dg_ref, cw_ref, cb_ref, wr_ref, br_ref, wi_ref, bi_ref, lam_ref,
                  y_ref, dtail_ref, hlast_ref, pad_ref, h_ref):
    tb = dx_ref.shape[0]

    @pl.when(pl.program_id(1) == 0)
    def _():
        pad_ref[0:PAD_ROWS, :] = jnp.zeros((PAD_ROWS, D_WIDTH), F32)
        h_ref[...] = jnp.zeros(h_ref.shape, F32)

    x = dx_ref[...]
    pad_ref[PAD_ROWS:PAD_ROWS + tb, :] = x
    cw = cw_ref[...]
    xc = pad_ref[PAD_ROWS - 3:PAD_ROWS - 3 + tb, :] * cw[0:1]
    xc = xc + pad_ref[PAD_ROWS - 2:PAD_ROWS - 2 + tb, :] * cw[1:2]
    xc = xc + pad_ref[PAD_ROWS - 1:PAD_ROWS - 1 + tb, :] * cw[2:3]
    xc = xc + x * cw[3:4]
    xc = xc + cb_ref[...]
    a, u = _lru_coeffs(xc, wr_ref[...], br_ref[...], wi_ref[...], bi_ref[...], lam_ref[...])
    row = lax.broadcasted_iota(jnp.int32, a.shape, 0)
    d = 1
    while d < tb:
        keep = row >= d
        a_prev = jnp.where(keep, pltpu.roll(a, d, 0), 1.0)
        u_prev = jnp.where(keep, pltpu.roll(u, d, 0), 0.0)
        u = a * u_prev + u
        a = a * a_prev
        d *= 2
    h = a * h_ref[0:1, :] + u
    y_ref[...] = (jax.nn.gelu(dg_ref[...]) * h).astype(y_ref.dtype)
    h_ref[0:1, :] = h[tb - 1:tb, :]
    hlast_ref[...] = h[tb - 1:tb, :]
    dtail_ref[...] = pad_ref[PAD_ROWS + tb - (D_CONV - 1):PAD_ROWS + tb, :]
    pad_ref[0:PAD_ROWS, :] = pad_ref[tb:tb + PAD_ROWS, :]


def _rglru_prompt(z3, cw, cb, wr, br, wi, bi, lam):
    bn, s_len, _ = z3.shape
    tb = min(256, s_len)
    nt = s_len // tb
    spec = lambda name: pl.BlockSpec((None, tb, 512), lambda b, t: (b, t, _blk(name, 512)))
    vec = pl.BlockSpec((1, D_WIDTH), lambda b, t: (0, 0))
    mat = pl.BlockSpec((D_WIDTH, D_WIDTH), lambda b, t: (0, 0))
    return pl.pallas_call(
        _rglru_kernel,
        grid=(bn, nt),
        in_specs=[spec("dx"), spec("dg"), pl.BlockSpec((D_CONV, D_WIDTH), lambda b, t: (0, 0)),
                  vec, mat, vec, mat, vec, vec],
        out_specs=[pl.BlockSpec((None, tb, D_WIDTH), lambda b, t: (b, t, 0)),
                   pl.BlockSpec((None, D_CONV - 1, D_WIDTH), lambda b, t: (b, 0, 0)),
                   pl.BlockSpec((None, 1, D_WIDTH), lambda b, t: (b, 0, 0))],
        out_shape=[jax.ShapeDtypeStruct((bn, s_len, D_WIDTH), BF16),
                   jax.ShapeDtypeStruct((bn, D_CONV - 1, D_WIDTH), F32),
                   jax.ShapeDtypeStruct((bn, 1, D_WIDTH), F32)],
        scratch_shapes=[pltpu.VMEM((PAD_ROWS + tb, D_WIDTH), F32),
                        pltpu.VMEM((8, D_WIDTH), F32)],
        compiler_params=_cparams(("parallel", "arbitrary")),
        name="rglru_prompt",
    )(z3, z3, cw, cb, wr, br, wi, bi, lam)


def _sample_bd_kernel(gb_ref, gc_ref, xb_ref, dg_ref, dx_ref, sb_ref, sd_ref, h0_ref,
                      bw_ref, cw_ref, cb_ref, wr_ref, br_ref, wi_ref, bi_ref, lam_ref,
                      yb_ref, yd_ref, bnew_ref, dnew_ref, hnew_ref):
    w = B_WIDTH
    p = gc_ref[...] * xb_ref[...]
    bw = bw_ref[...]
    b0 = sb_ref[:, 0:w]
    b1 = sb_ref[:, w:2 * w]
    y = b0 * bw[0:1] + b1 * bw[1:2] + p * bw[2:3]
    yb_ref[...] = (gb_ref[...] * y).astype(yb_ref.dtype)
    bnew_ref[:, 0:w] = b1
    bnew_ref[:, w:2 * w] = p

    w = D_WIDTH
    x = dx_ref[...]
    cw = cw_ref[...]
    d0 = sd_ref[:, 0:w]
    d1 = sd_ref[:, w:2 * w]
    d2 = sd_ref[:, 2 * w:3 * w]
    xc = d0 * cw[0:1]
    xc = xc + d1 * cw[1:2]
    xc = xc + d2 * cw[2:3]
    xc = xc + x * cw[3:4]
    xc = xc + cb_ref[...]
    a, u = _lru_coeffs(xc, wr_ref[...], br_ref[...], wi_ref[...], bi_ref[...], lam_ref[...])
    h = a * h0_ref[...] + u
    hnew_ref[...] = h
    yd_ref[...] = (jax.nn.gelu(dg_ref[...]) * h).astype(yd_ref.dtype)
    dnew_ref[:, 0:w] = d1
    dnew_ref[:, w:2 * w] = d2
    dnew_ref[:, 2 * w:3 * w] = x


def _sample_bd(z, sb, sd, h0, bw, cw, cb, wr, br, wi, bi, lam):
    m = z.shape[0]
    zspec = lambda name: pl.BlockSpec((m, 512), lambda i: (0, _blk(name, 512)))
    full = lambda a: pl.BlockSpec(a.shape, lambda i: (0,) * a.ndim)
    args = (sb, sd, h0, bw, cw, cb, wr, br, wi, bi, lam)
    out_shapes = [jax.ShapeDtypeStruct((m, B_WIDTH), BF16),
                  jax.ShapeDtypeStruct((m, D_WIDTH), BF16),
                  jax.ShapeDtypeStruct((m, (B_CONV - 1) * B_WIDTH), F32),
                  jax.ShapeDtypeStruct((m, (D_CONV - 1) * D_WIDTH), F32),
                  jax.ShapeDtypeStruct((m, D_WIDTH), F32)]
    return pl.pallas_call(
        _sample_bd_kernel,
        grid=(1,),
        in_specs=[zspec("gb"), zspec("gc"), zspec("xb"), zspec("dg"), zspec("dx")]
                 + [full(a) for a in args],
        out_specs=[pl.BlockSpec(s.shape, lambda i: (0, 0)) for s in out_shapes],
        out_shape=out_shapes,
        compiler_params=_cparams(("arbitrary",)),
        name="sample_bd",
    )(z, z, z, z, z, *args)


def _merge_kernel(ya_ref, yb_ref, yc_ref, yd_ref, g0_ref, g1_ref, g2_ref, g3_ref, x_ref,
                  wa_ref, wb_ref, wc_ref, wd_ref, wo_ref, gp_ref, o_ref):
    acc = jax.nn.sigmoid(g0_ref[...]) * _dot(ya_ref[...], wa_ref[...])
    acc = acc + jax.nn.sigmoid(g1_ref[...]) * _dot(yb_ref[...], wb_ref[...])
    acc = acc + jax.nn.sigmoid(g2_ref[...]) * _dot(yc_ref[...], wc_ref[...])
    acc = acc + jax.nn.sigmoid(g3_ref[...]) * _dot(yd_ref[...], wd_ref[...])
    m = _dot(acc.astype(BF16), wo_ref[...])
    o_ref[...] = x_ref[...] + _rms(m, gp_ref[...])


def _merge(ya, yb, yc, yd, z, x, wa, wb, wc, wd, wo, gp, tm):
    m = x.shape[0]
    tm = min(tm, m)
    row = lambda w: pl.BlockSpec((tm, w), lambda i: (i, 0))
    gate = lambda k: pl.BlockSpec((tm, D_MODEL), lambda i: (i, _blk("gt", D_MODEL) + k))
    const = lambda a: pl.BlockSpec(a.shape, lambda i: (0, 0))
    return pl.pallas_call(
        _merge_kernel,
        grid=(m // tm,),
        in_specs=[row(512), row(512), row(512), row(512), gate(0), gate(1), gate(2), gate(3),
                  row(D_MODEL), const(wa), const(wb), const(wc), const(wd), const(wo), const(gp)],
        out_specs=row(D_MODEL),
        out_shape=jax.ShapeDtypeStruct((m, D_MODEL), F32),
        compiler_params=_cparams(("parallel",)),
        name="merge",
    )(ya, yb, yc, yd, z, z, z, z, x, wa, wb, wc, wd, wo, gp)


def _ffn_kernel(x_ref, gpre_ref, wg_ref, wu_ref, wd_ref, gpost_ref, o_ref, h_ref, acc_ref, *, nf):
    j = pl.program_id(1)

    @pl.when(j == 0)
    def _():
        h_ref[...] = _rms(x_ref[...], gpre_ref[...]).astype(BF16)
        acc_ref[...] = jnp.zeros(acc_ref.shape, F32)

    h = h_ref[...]
    act = jax.nn.silu(_dot(h, wg_ref[...])) * _dot(h, wu_ref[...])
    acc_ref[...] += _dot(act.astype(BF16), wd_ref[...])

    @pl.when(j == nf - 1)
    def _():
        o_ref[...] = x_ref[...] + _rms(acc_ref[...], gpost_ref[...])


def _ffn(x, gpre, wg, wu, wd, gpost, tm, tf=256):
    m = x.shape[0]
    tm = min(tm, m)
    nf = D_FF // tf
    return pl.pallas_call(
        functools.partial(_ffn_kernel, nf=nf),
        grid=(m // tm, nf),
        in_specs=[pl.BlockSpec((tm, D_MODEL), lambda i, j: (i, 0)),
                  pl.BlockSpec((1, D_MODEL), lambda i, j: (0, 0)),
                  pl.BlockSpec((D_MODEL, tf), lambda i, j: (0, j)),
                  pl.BlockSpec((D_MODEL, tf), lambda i, j: (0, j)),
                  pl.BlockSpec((tf, D_MODEL), lambda i, j: (j, 0)),
                  pl.BlockSpec((1, D_MODEL), lambda i, j: (0, 0))],
        out_specs=pl.BlockSpec((tm, D_MODEL), lambda i, j: (i, 0)),
        out_shape=jax.ShapeDtypeStruct((m, D_MODEL), F32),
        scratch_shapes=[pltpu.VMEM((tm, D_MODEL), BF16), pltpu.VMEM((tm, D_MODEL), F32)],
        compiler_params=_cparams(("parallel", "arbitrary")),
        name="ffn",
    )(x, gpre, wg, wu, wd, gpost)


IDX_ROWS = 8


def _sample_scores_kernel(pt_ref, iq_ref, iw_ref, ikn_ref, *rest, g_pages, n_pages):
    ki_refs = rest[:g_pages]
    o_ref = rest[g_pages]
    j = pl.program_id(1)
    qi = iq_ref[...].astype(BF16)
    wi = iw_ref[...]
    for g in range(g_pages):
        rel = jnp.maximum(_dot_nt(qi, ki_refs[g][...].astype(BF16)), 0.0)
        sc = jnp.sum(wi * rel, axis=0, keepdims=True)
        start = pl.multiple_of((j * g_pages + g) * PAGE_SIZE, PAGE_SIZE)
        o_ref[:, pl.ds(start, PAGE_SIZE)] = sc

    @pl.when(j == n_pages // g_pages - 1)
    def _():
        kn = ikn_ref[...].astype(BF16).astype(F32)
        rel = jnp.maximum(jnp.sum(qi.astype(F32) * kn, axis=1, keepdims=True), 0.0)
        sc = jnp.sum(wi * rel, axis=0, keepdims=True)
        lane = lax.broadcasted_iota(jnp.int32, (1, PAGE_SIZE), 1)
        o_ref[:, n_pages * PAGE_SIZE:(n_pages + 1) * PAGE_SIZE] = jnp.where(lane == 0, sc, -jnp.inf)


def _sample_scores(pt_flat, iq3, iw3, ikn3, cache_ki, layer, n_pages, g_pages):
    bd = iq3.shape[0]
    ns = (n_pages + 1) * PAGE_SIZE

    def page_map(g):
        return lambda b, j, pt: (layer, pt[b * n_pages + j * g_pages + g], 0, 0)

    grid_spec = pltpu.PrefetchScalarGridSpec(
        num_scalar_prefetch=1,
        grid=(bd, n_pages // g_pages),
        in_specs=[pl.BlockSpec((None, IDX_ROWS, IDX_DIM), lambda b, j, pt: (b, 0, 0)),
                  pl.BlockSpec((None, IDX_ROWS, 1), lambda b, j, pt: (b, 0, 0)),
                  pl.BlockSpec((None, 1, IDX_DIM), lambda b, j, pt: (b, 0, 0))]
                 + [pl.BlockSpec((None, None, PAGE_SIZE, IDX_DIM), page_map(g)) for g in range(g_pages)],
        out_specs=pl.BlockSpec((None, 1, ns), lambda b, j, pt: (b, 0, 0)),
    )
    return pl.pallas_call(
        functools.partial(_sample_scores_kernel, g_pages=g_pages, n_pages=n_pages),
        grid_spec=grid_spec,
        out_shape=jax.ShapeDtypeStruct((bd, 1, ns), F32),
        compiler_params=_cparams(("parallel", "arbitrary")),
        name="sample_scores",
    )(pt_flat, iq3, iw3, ikn3, *([cache_ki] * g_pages))


def _sample_select_kernel(s_ref, o_ref, *, topk, nbits, n_valid):
    xs0 = s_ref[...]
    col = lax.broadcasted_iota(jnp.int32, xs0.shape, 1)
    allowed = col < n_valid
    xs = jnp.where(allowed, xs0 + 0.0, -jnp.inf)
    sel = _topk_select(xs, allowed, col, topk, nbits)
    o_ref[...] = jnp.where(sel, 1.0, 0.0)


def _sample_select(scores, topk, n_valid):
    bd, ns = scores.shape
    nbits = max(1, (ns - 1).bit_length())
    return pl.pallas_call(
        functools.partial(_sample_select_kernel, topk=topk, nbits=nbits, n_valid=n_valid),
        grid=(1,),
        in_specs=[pl.BlockSpec((bd, ns), lambda i: (0, 0))],
        out_specs=pl.BlockSpec((bd, ns), lambda i: (0, 0)),
        out_shape=jax.ShapeDtypeStruct((bd, ns), F32),
        compiler_params=_cparams(("arbitrary",)),
        name="sample_select",
    )(scores)


def _online_update(s, v_bf16, m_ref, l_ref, acc_ref):
    m_old = m_ref[...]
    m_new = jnp.maximum(m_old, jnp.max(s, axis=1, keepdims=True))
    alpha = jnp.exp(m_old - m_new)
    p = jnp.exp(s - m_new)
    l_ref[...] = alpha * l_ref[...] + _rowsum(p)
    acc_ref[...] = alpha * acc_ref[...] + _dot(p.astype(BF16), v_bf16)
    m_ref[...] = m_new


def _online_update_one(s, v_row, m_ref, l_ref, acc_ref):
    m_old = m_ref[...]
    m_new = jnp.maximum(m_old, s)
    alpha = jnp.exp(m_old - m_new)
    p = jnp.exp(s - m_new)
    l_ref[...] = alpha * l_ref[...] + p
    acc_ref[...] = alpha * acc_ref[...] + p.astype(BF16).astype(F32) * v_row
    m_ref[...] = m_new


def _sample_attn_kernel(pt_ref, mask_ref, qa_ref, ql_ref, qr_ref, kn_ref, vn_ref, cn_ref, rn_ref,
                        *rest, g_pages, n_pages):
    k_refs = rest[0:g_pages]
    v_refs = rest[g_pages:2 * g_pages]
    c_refs = rest[2 * g_pages:3 * g_pages]
    r_refs = rest[3 * g_pages:4 * g_pages]
    oa_ref, oc_ref, ma_ref, la_ref, acca_ref, mc_ref, lc_ref, accc_ref = rest[4 * g_pages:]
    j = pl.program_id(1)

    @pl.when(j == 0)
    def _():
        for m_ref, l_ref, acc_ref in ((ma_ref, la_ref, acca_ref), (mc_ref, lc_ref, accc_ref)):
            m_ref[...] = jnp.full(m_ref.shape, NEG_BIG, F32)
            l_ref[...] = jnp.zeros(l_ref.shape, F32)
            acc_ref[...] = jnp.zeros(acc_ref.shape, F32)

    qa = qa_ref[...].astype(BF16)
    ql = ql_ref[...]
    qr = qr_ref[...].astype(BF16)
    for g in range(g_pages):
        start = pl.multiple_of((j * g_pages + g) * PAGE_SIZE, PAGE_SIZE)
        member = mask_ref[:, pl.ds(start, PAGE_SIZE)] > 0.5
        s = _dot_nt(qa, k_refs[g][...].astype(BF16)) * A_SCALE
        s = jnp.where(member, s, -jnp.inf)
        _online_update(s, v_refs[g][...].astype(BF16), ma_ref, la_ref, acca_ref)
        ck = c_refs[g][...].astype(BF16)
        s = (_dot_nt(ql, ck) + _dot_nt(qr, r_refs[g][...].astype(BF16))) * C_SCALE
        _online_update(s, ck, mc_ref, lc_ref, accc_ref)

    @pl.when(j == n_pages // g_pages - 1)
    def _():
        rnd = lambda x: x.astype(BF16).astype(F32)
        member = mask_ref[:, n_pages * PAGE_SIZE:n_pages * PAGE_SIZE + 1] > 0.5
        s = jnp.sum(rnd(qa_ref[...]) * rnd(kn_ref[...]), axis=1, keepdims=True) * A_SCALE
        s = jnp.where(member, s, -jnp.inf)
        _online_update_one(s, rnd(vn_ref[...]), ma_ref, la_ref, acca_ref)
        cn = rnd(cn_ref[...])
        s = (jnp.sum(ql.astype(F32) * cn, axis=1, keepdims=True)
             + jnp.sum(rnd(qr_ref[...]) * rnd(rn_ref[...]), axis=1, keepdims=True)) * C_SCALE
        _online_update_one(s, cn, mc_ref, lc_ref, accc_ref)
        o = acca_ref[...] / la_ref[...]
        first_group = lax.broadcasted_iota(jnp.int32, oa_ref.shape, 0) < A_HEADS // A_KV_HEADS
        oa_ref[...] = jnp.where(first_group, o[:, :A_HEAD_DIM], o[:, A_HEAD_DIM:])
        oc_ref[...] = accc_ref[...] / lc_ref[...]


def _sample_attn(pt_flat, mask3, qa, ql, qr, kn, vn, cn, rn, cache_k, cache_v, cache_c, cache_r,
                 layer, n_pages, g_pages):
    bd = qa.shape[0]
    ns = mask3.shape[-1]

    def page_map(g):
        return lambda b, j, pt: (layer, pt[b * n_pages + j * g_pages + g], 0, 0)

    per_b = lambda r, w: pl.BlockSpec((None, r, w), lambda b, j, pt: (b, 0, 0))
    pages = lambda w: [pl.BlockSpec((None, None, PAGE_SIZE, w), page_map(g)) for g in range(g_pages)]
    grid_spec = pltpu.PrefetchScalarGridSpec(
        num_scalar_prefetch=1,
        grid=(bd, n_pages // g_pages),
        in_specs=[per_b(1, ns), per_b(A_HEADS, LANES), per_b(C_HEADS, C_KV_RANK), per_b(C_HEADS, C_ROPE),
                  per_b(1, LANES), per_b(1, LANES), per_b(1, C_KV_RANK), per_b(1, C_ROPE)]
                 + pages(LANES) + pages(LANES) + pages(C_KV_RANK) + pages(C_ROPE),
        out_specs=[per_b(A_HEADS, A_HEAD_DIM), per_b(C_HEADS, C_KV_RANK)],
        scratch_shapes=[pltpu.VMEM((A_HEADS, 1), F32), pltpu.VMEM((A_HEADS, 1), F32),
                        pltpu.VMEM((A_HEADS, LANES), F32),
                        pltpu.VMEM((C_HEADS, 1), F32), pltpu.VMEM((C_HEADS, 1), F32),
                        pltpu.VMEM((C_HEADS, C_KV_RANK), F32)],
    )
    return pl.pallas_call(
        functools.partial(_sample_attn_kernel, g_pages=g_pages, n_pages=n_pages),
        grid_spec=grid_spec,
        out_shape=[jax.ShapeDtypeStruct((bd, A_HEADS, A_HEAD_DIM), F32),
                   jax.ShapeDtypeStruct((bd, C_HEADS, C_KV_RANK), F32)],
        compiler_params=_cparams(("parallel", "arbitrary")),
        name="sample_attn",
    )(pt_flat, mask3, qa, ql, qr, kn, vn, cn, rn,
      *([cache_k] * g_pages), *([cache_v] * g_pages), *([cache_c] * g_pages), *([cache_r] * g_pages))


def _uv_kernel(o_ref, wuv_ref, y_ref):
    w = 2 * C_KV_RANK
    outs = [_dot(o_ref[:, w * p:w * (p + 1)].astype(BF16), wuv_ref[p]) for p in range(C_HEADS // 2)]
    y_ref[...] = jnp.concatenate(outs, axis=-1).astype(y_ref.dtype)


def _uv_proj(o_lat, wuvp):
    m = o_lat.shape[0]
    return pl.pallas_call(
        _uv_kernel,
        grid=(1,),
        in_specs=[pl.BlockSpec(o_lat.shape, lambda i: (0, 0)),
                  pl.BlockSpec(wuvp.shape, lambda i: (0, 0, 0))],
        out_specs=pl.BlockSpec((m, C_WIDTH), lambda i: (0, 0)),
        out_shape=jax.ShapeDtypeStruct((m, C_WIDTH), BF16),
        compiler_params=_cparams(("arbitrary",)),
        name="uv_proj",
    )(o_lat, wuvp)


def _rope_tables(pos):
    half = C_ROPE // 2
    freqs = ROPE_THETA ** (-jnp.arange(half, dtype=F32) / half)
    ang = pos.astype(F32)[:, None] * freqs[None, :]
    return jnp.tile(jnp.cos(ang), (1, C_HEADS)), jnp.tile(jnp.sin(ang), (1, C_HEADS))


def _layer_weights(l, w_in, w_out_a, conv_b, w_out_b, g_cq, w_uq, g_ckv, w_uk, w_uv, w_out_c,
                   conv_d_w, conv_d_b, w_rg, b_rg, w_ig, b_ig, lru_lambda, w_out_d, w_o,
                   w_ffn_gate, w_ffn_up, w_ffn_down):
    lw = {}
    lw["w_in"] = _relayout_w_in(w_in[l])
    uq = w_uq[l].reshape(C_Q_RANK, C_HEADS, C_NOPE + C_ROPE)
    half = C_ROPE // 2
    lw["w_uq"] = jnp.concatenate(
        [uq[:, :, :C_NOPE].reshape(C_Q_RANK, -1),
         uq[:, :, C_NOPE:C_NOPE + half].reshape(C_Q_RANK, -1),
         uq[:, :, C_NOPE + half:].reshape(C_Q_RANK, -1)], axis=1).astype(BF16)
    ukt = jnp.transpose(w_uk[l], (1, 2, 0))
    zk = jnp.zeros((C_NOPE, C_KV_RANK), F32)
    lw["w_uk"] = jnp.stack([
        jnp.concatenate([jnp.concatenate([ukt[2 * p], zk], axis=1),
                         jnp.concatenate([zk, ukt[2 * p + 1]], axis=1)], axis=0)
        for p in range(C_HEADS // 2)]).astype(BF16)
    uv = jnp.transpose(w_uv[l], (1, 0, 2))
    zv = jnp.zeros((C_KV_RANK, C_VDIM), F32)
    lw["w_uv"] = jnp.stack([
        jnp.concatenate([jnp.concatenate([uv[2 * p], zv], axis=1),
                         jnp.concatenate([zv, uv[2 * p + 1]], axis=1)], axis=0)
        for p in range(C_HEADS // 2)]).astype(BF16)
    eye = jnp.eye(D_BLOCKS, dtype=F32)
    bd = lambda w: (eye[:, None, :, None] * w[:, :, None, :]).reshape(D_WIDTH, D_WIDTH).astype(BF16)
    lw["w_rg"] = bd(w_rg[l])
    lw["w_ig"] = bd(w_ig[l])
    row = lambda v: v[l][None, :]
    lw["b_rg"], lw["b_ig"], lw["lam"], lw["conv_d_b"] = row(b_rg), row(b_ig), row(lru_lambda), row(conv_d_b)
    lw["g_cq"], lw["g_ckv"] = row(g_cq), row(g_ckv)
    lw["conv_b"], lw["conv_d_w"] = conv_b[l], conv_d_w[l]
    for name, w in (("w_out_a", w_out_a), ("w_out_b", w_out_b), ("w_out_c", w_out_c),
                    ("w_out_d", w_out_d), ("w_o", w_o), ("w_ffn_gate", w_ffn_gate),
                    ("w_ffn_up", w_ffn_up), ("w_ffn_down", w_ffn_down)):
        lw[name] = w[l].astype(BF16)
    return lw


def _zcols(z, name, lo, hi):
    o = _DST[name][0]
    return z[..., o + lo:o + hi]


def _state_from_z(z):
    hd = A_HEAD_DIM
    ak = jnp.concatenate([_zcols(z, "ak", 0, hd), _zcols(z, "ak", 2 * hd, 3 * hd)], axis=-1)
    av = jnp.concatenate([_zcols(z, "av", 0, hd), _zcols(z, "av", 2 * hd, 3 * hd)], axis=-1)
    ik = _zcols(z, "ik", 0, IDX_DIM)
    return ak, av, ik


def _krope_state(kr):
    half = C_ROPE // 2
    return jnp.concatenate([kr[:, :half], kr[:, LANES:LANES + half]], axis=-1)


def _prompt_layer(x, lw, gains, bn, s_len, cos, sin):
    g_mix_pre, g_mix_post, g_ffn_pre, g_ffn_post = gains
    m = bn * s_len
    z = _inproj(x, g_mix_pre, lw["w_in"], tm=512)
    z3 = z.reshape(bn, s_len, N_PROJ)
    tm_c = min(512, s_len)
    nb = s_len // tm_c
    q, kcat, ckvn, kr, _ = _cprep(z, cos, sin, lambda i: (i % nb, 0), lw["g_cq"], lw["g_ckv"],
                                  lw["w_uq"], lw["w_uk"], tm=tm_c)
    yc = _mla_prompt(q, kcat, lw["w_uv"], bn, s_len)
    ya = _sparse_prompt(z3, min(TOPK_MAX, s_len // 4)).reshape(m, A_WIDTH)
    yb, b_new = _bconv_prompt(z3, lw["conv_b"])
    yd, d_new, h_new = _rglru_prompt(z3, lw["conv_d_w"], lw["conv_d_b"], lw["w_rg"], lw["b_rg"],
                                     lw["w_ig"], lw["b_ig"], lw["lam"])
    x = _merge(ya, yb.reshape(m, B_WIDTH), yc, yd.reshape(m, D_WIDTH), z, x,
               lw["w_out_a"], lw["w_out_b"], lw["w_out_c"], lw["w_out_d"], lw["w_o"], g_mix_post, tm=512)
    x = _ffn(x, g_ffn_pre, lw["w_ffn_gate"], lw["w_ffn_up"], lw["w_ffn_down"], g_ffn_post, tm=512)
    ak, av, ik = _state_from_z(z3)
    st = (ak.reshape(bn, s_len, A_KV_HEADS, A_HEAD_DIM), av.reshape(bn, s_len, A_KV_HEADS, A_HEAD_DIM),
          ik, ckvn.reshape(bn, s_len, C_KV_RANK), _krope_state(kr).reshape(bn, s_len, C_ROPE),
          b_new, d_new, h_new.reshape(bn, D_WIDTH))
    return x, st


def _sample_layer(x, lw, gains, layer, caches, states, pt_flat, n_pages, cos, sin, g_pages):
    g_mix_pre, g_mix_post, g_ffn_pre, g_ffn_post = gains
    cache_k, cache_v, cache_ki, cache_c, cache_r = caches
    sb, sd, h0 = states
    bd = x.shape[0]
    z = _inproj(x, g_mix_pre, lw["w_in"], tm=bd)
    q, _, ckvn, kr, qrope = _cprep(z, cos, sin, lambda i: (0, 0), lw["g_cq"], lw["g_ckv"],
                                   lw["w_uq"], lw["w_uk"], tm=bd)
    ak, av, ik = _state_from_z(z)
    krope = _krope_state(kr)

    iq3 = jnp.pad(_zcols(z, "iq", 0, 256).reshape(bd, IDX_HEADS, IDX_DIM),
                  ((0, 0), (0, IDX_ROWS - IDX_HEADS), (0, 0)))
    iw3 = jnp.pad(_zcols(z, "iw", 0, IDX_HEADS), ((0, 0), (0, IDX_ROWS - IDX_HEADS)))[:, :, None]
    scores = _sample_scores(pt_flat, iq3, iw3, ik[:, None, :], cache_ki, layer, n_pages, g_pages)
    total = n_pages * PAGE_SIZE + 1
    mask = _sample_select(scores.reshape(bd, -1), min(TOPK_MAX, total // 4), total)

    aq = _zcols(z, "aq", 0, A_WIDTH).reshape(bd, A_KV_HEADS, A_HEADS // A_KV_HEADS, A_HEAD_DIM)
    zq = jnp.zeros_like(aq[:, 0])
    qa = jnp.concatenate([jnp.concatenate([aq[:, 0], zq], axis=-1),
                          jnp.concatenate([zq, aq[:, 1]], axis=-1)], axis=1)
    ql = jnp.transpose(q[:, :, :C_KV_RANK], (1, 0, 2))
    half = C_ROPE // 2
    qr = jnp.concatenate([qrope[:, :LANES].reshape(bd, C_HEADS, half),
                          qrope[:, LANES:].reshape(bd, C_HEADS, half)], axis=-1)
    oa, oc = _sample_attn(pt_flat, mask[:, None, :], qa, ql, qr, ak[:, None, :], av[:, None, :],
                          ckvn[:, None, :], krope[:, None, :], cache_k, cache_v, cache_c, cache_r,
                          layer, n_pages, g_pages)
    ya = oa.reshape(bd, A_WIDTH).astype(BF16)
    yc = _uv_proj(oc.reshape(bd, C_HEADS * C_KV_RANK), lw["w_uv"])

    yb, yd, b_new, d_new, h_new = _sample_bd(
        z, sb.reshape(bd, -1), sd.reshape(bd, -1), h0, lw["conv_b"], lw["conv_d_w"], lw["conv_d_b"],
        lw["w_rg"], lw["b_rg"], lw["w_ig"], lw["b_ig"], lw["lam"])
    x = _merge(ya, yb, yc, yd, z, x, lw["w_out_a"], lw["w_out_b"], lw["w_out_c"], lw["w_out_d"],
               lw["w_o"], g_mix_post, tm=bd)
    x = _ffn(x, g_ffn_pre, lw["w_ffn_gate"], lw["w_ffn_up"], lw["w_ffn_down"], g_ffn_post, tm=bd)
    st = (ak.reshape(bd, 1, A_KV_HEADS, A_HEAD_DIM), av.reshape(bd, 1, A_KV_HEADS, A_HEAD_DIM),
          ik[:, None, :], ckvn[:, None, :], krope[:, None, :],
          b_new.reshape(bd, B_CONV - 1, B_WIDTH), d_new.reshape(bd, D_CONV - 1, D_WIDTH), h_new)
    return x, st


def kernel(x_prompt, x_sample, cache_a_k, cache_a_v, cache_a_kidx, cache_c_kv, cache_c_krope, state_b_conv, state_d_conv, state_d_h, page_table, g_mix_pre, g_mix_post, g_ffn_pre, g_ffn_post, w_in, w_out_a, conv_b, w_out_b, g_cq, w_uq, g_ckv, w_uk, w_uv, w_out_c, conv_d_w, conv_d_b, w_rg, b_rg, w_ig, b_ig, lru_lambda, w_out_d, w_o, w_ffn_gate, w_ffn_up, w_ffn_down):
    bn, s_len, _ = x_prompt.shape
    bd, t_len, _ = x_sample.shape
    assert t_len == 1 and s_len % Q_BLOCK == 0
    depth = w_in.shape[0]
    n_pages = page_table.shape[1]
    n_pool = cache_a_k.shape[1]
    past = n_pages * PAGE_SIZE
    g_pages = 4 if n_pages % 4 == 0 else 1

    cos_p, sin_p = _rope_tables(jnp.arange(s_len))
    cos_s, sin_s = _rope_tables(jnp.full((bd,), past))
    pt_flat = page_table.reshape(-1)
    caches = (cache_a_k.reshape(depth, n_pool, PAGE_SIZE, A_KV_HEADS * A_HEAD_DIM),
              cache_a_v.reshape(depth, n_pool, PAGE_SIZE, A_KV_HEADS * A_HEAD_DIM),
              cache_a_kidx, cache_c_kv, cache_c_krope)

    xp = x_prompt.reshape(bn * s_len, D_MODEL)
    xs = x_sample.reshape(bd, D_MODEL)
    p_states, s_states = [], []
    for l in range(depth):
        lw = _layer_weights(l, w_in, w_out_a, conv_b, w_out_b, g_cq, w_uq, g_ckv, w_uk, w_uv, w_out_c,
                            conv_d_w, conv_d_b, w_rg, b_rg, w_ig, b_ig, lru_lambda, w_out_d, w_o,
                            w_ffn_gate, w_ffn_up, w_ffn_down)
        gains = tuple(g[l][None, :] for g in (g_mix_pre, g_mix_post, g_ffn_pre, g_ffn_post))
        xp, st_p = _prompt_layer(xp, lw, gains, bn, s_len, cos_p, sin_p)
        xs, st_s = _sample_layer(xs, lw, gains, l, caches,
                                 (state_b_conv[l], state_d_conv[l], state_d_h[l]),
                                 pt_flat, n_pages, cos_s, sin_s, g_pages)
        p_states.append(st_p)
        s_states.append(st_s)
    p_out = [jnp.stack([st[j] for st in p_states]) for j in range(8)]
    s_out = [jnp.stack([st[j] for st in s_states]) for j in range(8)]
    return (xp.reshape(bn, s_len, D_MODEL), xs.reshape(bd, 1, D_MODEL), *p_out, *s_out)
```

```python
import functools

import numpy as np
import jax
import jax.numpy as jnp
from jax import lax
from jax.experimental import pallas as pl
from jax.experimental.pallas import tpu as pltpu

D_MODEL = 1024
PAGE_SIZE = 128
A_HEADS = 8
A_KV_HEADS = 2
A_HEAD_DIM = 64
A_WIDTH = A_HEADS * A_HEAD_DIM
A_SCALE = A_HEAD_DIM ** -0.5
IDX_HEADS = 4
IDX_DIM = 64
TOPK_MAX = 256
B_WIDTH = 512
B_CONV = 3
C_HEADS = 8
C_NOPE = 64
C_ROPE = 32
C_VDIM = 64
C_Q_RANK = 256
C_KV_RANK = 256
C_WIDTH = C_HEADS * C_VDIM
C_SCALE = (C_NOPE + C_ROPE) ** -0.5
ROPE_THETA = 10000.0
D_WIDTH = 512
D_BLOCKS = 8
D_BLOCK = D_WIDTH // D_BLOCKS
D_CONV = 4
LRU_C = 8.0
N_BRANCH = 4
D_FF = ((8 * D_MODEL + 3 * 256 - 1) // (3 * 256)) * 256
Q_BLOCK = 128
EPS = 1e-6

LANES = 128
VMEM_LIMIT = 56 * 1024 * 1024
NEG_BIG = -1e30

BF16 = jnp.bfloat16
F32 = jnp.float32

_SRC = {}
_o = 0
for _n, _w in (("aq", A_WIDTH), ("ak", 128), ("av", 128), ("iq", 256), ("ik", 64), ("iw", 4),
               ("gb", 512), ("gc", 512), ("xb", 512), ("cq", 256), ("ckv", 256), ("ckr", 32),
               ("dg", 512), ("dx", 512), ("gt", 4096)):
    _SRC[_n] = (_o, _w)
    _o += _w
N_IN = _o

_DST = {}
_o = 0
for _n, _w in (("aq", 512), ("gb", 512), ("gc", 512), ("xb", 512), ("dg", 512), ("dx", 512),
               ("gt", 4096), ("iq", 256), ("cq", 256), ("ckv", 256), ("ckr", 256),
               ("ak", 256), ("av", 256), ("ik", 128), ("iw", 128)):
    _DST[_n] = (_o, _w)
    _o += _w
N_PROJ = _o
PROJ_TN = 1280


def _blk(name, width):
    off = _DST[name][0]
    assert off % width == 0
    return off // width


def _cparams(sem, vmem=VMEM_LIMIT):
    return pltpu.CompilerParams(dimension_semantics=sem, vmem_limit_bytes=vmem)


def _relayout_w_in(w):
    def src(name, lo=0, hi=None):
        o, wd = _SRC[name]
        hi = wd if hi is None else hi
        return w[:, o + lo:o + hi]

    zeros = lambda n: jnp.zeros((w.shape[0], n), w.dtype)
    parts = [src("aq"), src("gb"), src("gc"), src("xb"), src("dg"), src("dx"), src("gt"),
             src("iq"), src("cq"), src("ckv"),
             jnp.tile(src("ckr", 0, 16), (1, 8)), jnp.tile(src("ckr", 16, 32), (1, 8)),
             src("ak", 0, 64), src("ak", 0, 64), src("ak", 64, 128), src("ak", 64, 128),
             src("av", 0, 64), src("av", 0, 64), src("av", 64, 128), src("av", 64, 128),
             src("ik"), src("ik"),
             src("iw"), zeros(124)]
    out = jnp.concatenate(parts, axis=1)
    assert out.shape[1] == N_PROJ
    return out.astype(BF16)


def _rms(x, g):
    inv = lax.rsqrt(jnp.mean(x * x, axis=-1, keepdims=True) + EPS)
    return (x * inv) * g


def _dot(a, b):
    return jnp.dot(a, b, preferred_element_type=F32)


def _dot_nt(a, b):
    return lax.dot_general(a, b, (((1,), (1,)), ((), ())), preferred_element_type=F32)


def _inproj_kernel(x_ref, g_ref, w_ref, o_ref, h_ref):
    @pl.when(pl.program_id(1) == 0)
    def _():
        h_ref[...] = _rms(x_ref[...], g_ref[...]).astype(BF16)

    o_ref[...] = _dot(h_ref[...], w_ref[...])


def _inproj(x, g, w, tm):
    m = x.shape[0]
    tm = min(tm, m)
    return pl.pallas_call(
        _inproj_kernel,
        grid=(m // tm, N_PROJ // PROJ_TN),
        in_specs=[pl.BlockSpec((tm, D_MODEL), lambda i, j: (i, 0)),
                  pl.BlockSpec((1, D_MODEL), lambda i, j: (0, 0)),
                  pl.BlockSpec((D_MODEL, PROJ_TN), lambda i, j: (0, j))],
        out_specs=pl.BlockSpec((tm, PROJ_TN), lambda i, j: (i, j)),
        out_shape=jax.ShapeDtypeStruct((m, N_PROJ), F32),
        scratch_shapes=[pltpu.VMEM((tm, D_MODEL), BF16)],
        compiler_params=_cparams(("parallel", "arbitrary")),
        name="in_proj",
    )(x, g, w)


def _cprep_kernel(cq_ref, ckv_ref, ckr_ref, cos_ref, sin_ref, gq_ref, gkv_ref, wuq_ref, wuk_ref,
                  q_ref, kcat_ref, ckvn_ref, kr_ref, qr_ref):
    cos = cos_ref[...]
    sin = sin_ref[...]
    cqn = _rms(cq_ref[...], gq_ref[...]).astype(BF16)
    q = _dot(cqn, wuq_ref[...])
    nope = C_HEADS * C_NOPE
    r1 = q[:, nope:nope + LANES]
    r2 = q[:, nope + LANES:nope + 2 * LANES]
    o1 = r1 * cos - r2 * sin
    o2 = r1 * sin + r2 * cos
    qr_ref[...] = jnp.concatenate([o1, o2], axis=-1)
    head_of_lane = lax.broadcasted_iota(jnp.int32, o1.shape, 1) // (C_ROPE // 2)
    for p in range(C_HEADS // 2):
        ql = _dot(q[:, LANES * p:LANES * (p + 1)].astype(BF16), wuk_ref[p])
        for e in range(2):
            h = 2 * p + e
            sel = head_of_lane == h
            q_ref[h] = jnp.concatenate(
                [ql[:, C_KV_RANK * e:C_KV_RANK * (e + 1)],
                 jnp.where(sel, o1, 0.0), jnp.where(sel, o2, 0.0)], axis=-1).astype(BF16)
    ckvn = _rms(ckv_ref[...], gkv_ref[...])
    ckvn_ref[...] = ckvn
    c = ckr_ref[...]
    c1 = c[:, :LANES]
    c2 = c[:, LANES:]
    k1 = c1 * cos - c2 * sin
    k2 = c1 * sin + c2 * cos
    kr_ref[...] = jnp.concatenate([k1, k2], axis=-1)
    kcat_ref[...] = jnp.concatenate([ckvn, k1, k2], axis=-1).astype(BF16)


def _cprep(z, cos, sin, tab_map, g_cq, g_ckv, wuq, wukp, tm):
    m = z.shape[0]
    tm = min(tm, m)
    qw = C_KV_RANK + 2 * LANES
    const2 = lambda i: (0, 0)
    return pl.pallas_call(
        _cprep_kernel,
        grid=(m // tm,),
        in_specs=[pl.BlockSpec((tm, 256), lambda i: (i, _blk("cq", 256))),
                  pl.BlockSpec((tm, 256), lambda i: (i, _blk("ckv", 256))),
                  pl.BlockSpec((tm, 256), lambda i: (i, _blk("ckr", 256))),
                  pl.BlockSpec((tm, LANES), tab_map),
                  pl.BlockSpec((tm, LANES), tab_map),
                  pl.BlockSpec((1, C_Q_RANK), const2),
                  pl.BlockSpec((1, C_KV_RANK), const2),
                  pl.BlockSpec(wuq.shape, const2),
                  pl.BlockSpec(wukp.shape, lambda i: (0, 0, 0))],
        out_specs=[pl.BlockSpec((C_HEADS, tm, qw), lambda i: (0, i, 0)),
                   pl.BlockSpec((tm, qw), lambda i: (i, 0)),
                   pl.BlockSpec((tm, C_KV_RANK), lambda i: (i, 0)),
                   pl.BlockSpec((tm, 2 * LANES), lambda i: (i, 0)),
                   pl.BlockSpec((tm, 2 * LANES), lambda i: (i, 0))],
        out_shape=[jax.ShapeDtypeStruct((C_HEADS, m, qw), BF16),
                   jax.ShapeDtypeStruct((m, qw), BF16),
                   jax.ShapeDtypeStruct((m, C_KV_RANK), F32),
                   jax.ShapeDtypeStruct((m, 2 * LANES), F32),
                   jax.ShapeDtypeStruct((m, 2 * LANES), F32)],
        compiler_params=_cparams(("parallel",)),
        name="c_prep",
    )(z, z, z, cos, sin, g_cq, g_ckv, wuq, wukp)


MLA_CHUNK_HEADS = 2


def _mla_kernel(q_ref, k_ref, wuv_ref, o_ref, m_ref, l_ref, acc_ref, *, tk, nkb):
    i = pl.program_id(1)
    j = pl.program_id(2)
    rows = C_HEADS * Q_BLOCK

    @pl.when(j == 0)
    def _():
        m_ref[...] = jnp.full(m_ref.shape, NEG_BIG, F32)
        l_ref[...] = jnp.zeros(l_ref.shape, F32)
        acc_ref[...] = jnp.zeros(acc_ref.shape, F32)

    def step(masked):
        k = k_ref[...]
        v = k[:, :C_KV_RANK]
        for c in range(C_HEADS // MLA_CHUNK_HEADS):
            r0 = c * MLA_CHUNK_HEADS * Q_BLOCK
            r1 = r0 + MLA_CHUNK_HEADS * Q_BLOCK
            q = q_ref[c * MLA_CHUNK_HEADS:(c + 1) * MLA_CHUNK_HEADS].reshape(r1 - r0, q_ref.shape[-1])
            s = _dot_nt(q, k) * C_SCALE
            if masked:
                t = i * Q_BLOCK + lax.broadcasted_iota(jnp.int32, s.shape, 0) % Q_BLOCK
                col = j * tk + lax.broadcasted_iota(jnp.int32, s.shape, 1)
                s = jnp.where(col <= t, s, -jnp.inf)
            m_old = m_ref[r0:r1, :]
            m_new = jnp.maximum(m_old, jnp.max(s, axis=1, keepdims=True))
            alpha = jnp.exp(m_old - m_new)
            p = jnp.exp(s - m_new)
            l_ref[r0:r1, :] = alpha * l_ref[r0:r1, :] + jnp.sum(p, axis=1, keepdims=True)
            acc_ref[r0:r1, :] = alpha * acc_ref[r0:r1, :] + _dot(p.astype(BF16), v)
            m_ref[r0:r1, :] = m_new

    first_row = i * Q_BLOCK
    fully_visible = j * tk + tk - 1 <= first_row
    pl.when(fully_visible)(functools.partial(step, False))
    pl.when(jnp.logical_and(jnp.logical_not(fully_visible), j * tk <= first_row + Q_BLOCK - 1))(
        functools.partial(step, True))

    @pl.when(j == nkb - 1)
    def _():
        o = acc_ref[...] / l_ref[...]
        outs = []
        for p in range(C_HEADS // 2):
            pair = jnp.concatenate([o[2 * p * Q_BLOCK:(2 * p + 1) * Q_BLOCK],
                                    o[(2 * p + 1) * Q_BLOCK:(2 * p + 2) * Q_BLOCK]], axis=-1)
            outs.append(_dot(pair.astype(BF16), wuv_ref[p]))
        o_ref[...] = jnp.concatenate(outs, axis=-1).astype(o_ref.dtype)


def _mla_prompt(q, kcat, wuvp, bn, s_len):
    nq = s_len // Q_BLOCK
    tk = min(256, s_len)
    nkb = s_len // tk
    qw = q.shape[-1]

    def k_map(b, i, j):
        last = (i * Q_BLOCK + Q_BLOCK - 1) // tk
        return (b * nkb + jnp.minimum(j, last), 0)

    return pl.pallas_call(
        functools.partial(_mla_kernel, tk=tk, nkb=nkb),
        grid=(bn, nq, nkb),
        in_specs=[pl.BlockSpec((C_HEADS, Q_BLOCK, qw), lambda b, i, j: (0, b * nq + i, 0)),
                  pl.BlockSpec((tk, qw), k_map),
                  pl.BlockSpec(wuvp.shape, lambda b, i, j: (0, 0, 0))],
        out_specs=pl.BlockSpec((Q_BLOCK, C_WIDTH), lambda b, i, j: (b * nq + i, 0)),
        out_shape=jax.ShapeDtypeStruct((bn * s_len, C_WIDTH), BF16),
        scratch_shapes=[pltpu.VMEM((C_HEADS * Q_BLOCK, 1), F32),
                        pltpu.VMEM((C_HEADS * Q_BLOCK, 1), F32),
                        pltpu.VMEM((C_HEADS * Q_BLOCK, C_KV_RANK), F32)],
        compiler_params=_cparams(("parallel", "parallel", "arbitrary")),
        name="mla_prompt",
    )(q, kcat, wuvp)


def _rowsum(x):
    return jnp.sum(x, axis=1, keepdims=True)


def _topk_select(xs, allowed, col, k, nbits, n_bisect):
    kf = float(k)
    inf = jnp.float32(jnp.inf)
    row_min = jnp.min(jnp.where(allowed, xs, inf), axis=1, keepdims=True)
    row_max = jnp.max(xs, axis=1, keepdims=True)
    n_allowed = _rowsum(jnp.where(allowed, 1.0, 0.0))

    def bisect(_, c):
        low, high, has_low = c
        mid = low + (high - low) * 0.5
        up = _rowsum(jnp.where(xs > mid, 1.0, 0.0)) >= kf
        return jnp.where(up, mid, low), jnp.where(up, high, mid), jnp.where(up, 1.0, has_low)

    low, high, has_low = lax.fori_loop(0, n_bisect, bisect, (row_min, row_max, jnp.zeros_like(row_min)))
    lo0 = jnp.min(jnp.where(xs > low, xs, inf), axis=1, keepdims=True)
    lo0 = jnp.where((has_low > 0.5) & (n_allowed > kf), lo0, row_min)
    ub0 = jnp.max(jnp.where(xs > high, -inf, xs), axis=1, keepdims=True)
    ub0 = jnp.where(n_allowed > kf, ub0, row_min)

    def cond(c):
        lo, ub = c
        return jnp.max(jnp.where(lo < ub, 1.0, 0.0)) > 0.0

    def body(c):
        lo, ub = c
        mid = lo + (ub - lo) * 0.5
        mid = jnp.where((mid >= lo) & (mid < ub), mid, lo)
        gt = xs > mid
        cnt = _rowsum(jnp.where(gt, 1.0, 0.0))
        vmin = jnp.min(jnp.where(gt, xs, inf), axis=1, keepdims=True)
        vmax = jnp.max(jnp.where(gt, -inf, xs), axis=1, keepdims=True)
        active = lo < ub
        up = cnt >= kf
        lo = jnp.where(active & up, vmin, lo)
        ub = jnp.where(active & jnp.logical_not(up), vmax, ub)
        return lo, ub

    v, _ = lax.while_loop(cond, body, (lo0, ub0))
    gtv = xs > v
    need = kf - _rowsum(jnp.where(gtv, 1.0, 0.0))
    tie = allowed & (xs == v)
    n_tie = _rowsum(jnp.where(tie, 1.0, 0.0))

    def search(_):
        def step(it, c):
            cand = c + jnp.left_shift(jnp.int32(1), nbits - 1 - it)
            f = _rowsum(jnp.where(tie & (col < cand), 1.0, 0.0))
            return jnp.where(f < need, cand, c)
        return lax.fori_loop(0, nbits, step, jnp.zeros(v.shape, jnp.int32))

    def no_search(_):
        return jnp.full(v.shape, (1 << nbits) - 1, jnp.int32)

    any_over = jnp.max(jnp.where(n_tie > need, 1.0, 0.0)) > 0.0
    cut = lax.cond(any_over, search, no_search, 0)
    return gtv | (tie & (col <= cut))


KEY_CLASS = 256
N_BISECT_PROMPT = 14
N_BISECT_SAMPLE = 16


def _sparse_prompt_kernel(q_ref, iq_ref, iw_ref, kd_ref, vd_ref, ikd_ref, o_ref, kb_ref, vb_ref, ib_ref,
                          *, topk, s_len):
    i = pl.program_id(1)

    @pl.when(i == 0)
    def _():
        ib_ref[...] = ikd_ref[...].astype(BF16)
        for g in range(A_KV_HEADS):
            kb_ref[g] = kd_ref[:, LANES * g:LANES * (g + 1)].astype(BF16)
            vb_ref[g] = vd_ref[:, LANES * g:LANES * (g + 1)].astype(BF16)

    first_half = lax.broadcasted_iota(jnp.int32, (Q_BLOCK, LANES), 1) < (LANES // 2)

    def half(x, pair, e):
        blk = x[:, LANES * pair:LANES * (pair + 1)]
        return jnp.where(first_half if e == 0 else jnp.logical_not(first_half), blk, 0.0).astype(BF16)

    def body(width):
        q = q_ref[...] * A_SCALE
        iq = iq_ref[...]
        iw = iw_ref[...]
        ikd = ib_ref[0:width, :]
        score = None
        for h in range(IDX_HEADS):
            rel = jnp.maximum(_dot_nt(half(iq, h // 2, h % 2), ikd), 0.0)
            term = iw[:, h:h + 1] * rel
            score = term if score is None else score + term
        t = i * Q_BLOCK + lax.broadcasted_iota(jnp.int32, score.shape, 0)
        col = lax.broadcasted_iota(jnp.int32, score.shape, 1)
        allowed = col <= t
        xs = jnp.where(allowed, score + 0.0, -jnp.inf)
        nbits = max(1, (width - 1).bit_length())
        sel = _topk_select(xs, allowed, col, topk, nbits, N_BISECT_PROMPT)
        bias = jnp.where(sel, 0.0, -jnp.inf)

        rep = A_HEADS // A_KV_HEADS
        outs = []
        for g in range(A_KV_HEADS):
            kd = kb_ref[g, 0:width, :]
            vd = vb_ref[g, 0:width, :]
            for r in range(rep):
                h = g * rep + r
                s = _dot_nt(half(q, h // 2, h % 2), kd) + bias
                m = jnp.max(s, axis=1, keepdims=True)
                p = jnp.exp(s - m)
                l = _rowsum(p)
                outs.append(_dot(p.astype(BF16), vd) / l)
        pairs = [jnp.where(first_half, outs[2 * j], outs[2 * j + 1]) for j in range(A_HEADS // 2)]
        o_ref[...] = jnp.concatenate(pairs, axis=-1).astype(o_ref.dtype)

    per_class = KEY_CLASS // Q_BLOCK
    n_class = max(1, s_len // KEY_CLASS)
    for c in range(n_class):
        width = min(s_len, (c + 1) * KEY_CLASS)
        pl.when(i // per_class == c)(functools.partial(body, width))


def _sparse_prompt(z3, topk):
    bn, s_len, _ = z3.shape
    nq = s_len // Q_BLOCK
    return pl.pallas_call(
        functools.partial(_sparse_prompt_kernel, topk=topk, s_len=s_len),
        grid=(bn, nq),
        in_specs=[pl.BlockSpec((None, Q_BLOCK, 512), lambda b, i: (b, i, _blk("aq", 512))),
                  pl.BlockSpec((None, Q_BLOCK, 256), lambda b, i: (b, i, _blk("iq", 256))),
                  pl.BlockSpec((None, Q_BLOCK, 128), lambda b, i: (b, i, _blk("iw", 128))),
                  pl.BlockSpec((None, s_len, 256), lambda b, i: (b, 0, _blk("ak", 256))),
                  pl.BlockSpec((None, s_len, 256), lambda b, i: (b, 0, _blk("av", 256))),
                  pl.BlockSpec((None, s_len, 128), lambda b, i: (b, 0, _blk("ik", 128)))],
        out_specs=pl.BlockSpec((None, Q_BLOCK, A_WIDTH), lambda b, i: (b, i, 0)),
        out_shape=jax.ShapeDtypeStruct((bn, s_len, A_WIDTH), BF16),
        scratch_shapes=[pltpu.VMEM((A_KV_HEADS, s_len, LANES), BF16),
                        pltpu.VMEM((A_KV_HEADS, s_len, LANES), BF16),
                        pltpu.VMEM((s_len, LANES), BF16)],
        compiler_params=_cparams(("parallel", "arbitrary")),
        name="sparse_prompt",
    )(z3, z3, z3, z3, z3, z3)


PAD_ROWS = 8


def _bconv_kernel(gb_ref, gc_ref, xb_ref, w_ref, y_ref, tail_ref, pad_ref):
    tb = gb_ref.shape[0]

    @pl.when(pl.program_id(1) == 0)
    def _():
        pad_ref[0:PAD_ROWS, :] = jnp.zeros((PAD_ROWS, B_WIDTH), F32)

    p = gc_ref[...] * xb_ref[...]
    pad_ref[PAD_ROWS:PAD_ROWS + tb, :] = p
    w = w_ref[...]
    y = (pad_ref[PAD_ROWS - 2:PAD_ROWS - 2 + tb, :] * w[0:1]
         + pad_ref[PAD_ROWS - 1:PAD_ROWS - 1 + tb, :] * w[1:2]
         + p * w[2:3])
    y_ref[...] = (gb_ref[...] * y).astype(y_ref.dtype)
    tail_ref[...] = pad_ref[PAD_ROWS + tb - (B_CONV - 1):PAD_ROWS + tb, :]
    pad_ref[0:PAD_ROWS, :] = pad_ref[tb:tb + PAD_ROWS, :]


def _bconv_prompt(z3, conv_w):
    bn, s_len, _ = z3.shape
    tb = min(256, s_len)
    nt = s_len // tb
    spec = lambda name: pl.BlockSpec((None, tb, 512), lambda b, t: (b, t, _blk(name, 512)))
    return pl.pallas_call(
        _bconv_kernel,
        grid=(bn, nt),
        in_specs=[spec("gb"), spec("gc"), spec("xb"),
                  pl.BlockSpec((B_CONV, B_WIDTH), lambda b, t: (0, 0))],
        out_specs=[pl.BlockSpec((None, tb, B_WIDTH), lambda b, t: (b, t, 0)),
                   pl.BlockSpec((None, B_CONV - 1, B_WIDTH), lambda b, t: (b, 0, 0))],
        out_shape=[jax.ShapeDtypeStruct((bn, s_len, B_WIDTH), BF16),
                   jax.ShapeDtypeStruct((bn, B_CONV - 1, B_WIDTH), F32)],
        scratch_shapes=[pltpu.VMEM((PAD_ROWS + tb, B_WIDTH), F32)],
        compiler_params=_cparams(("parallel", "arbitrary")),
        name="bconv_prompt",
    )(z3, z3, z3, conv_w)


def _lru_coeffs(xc, wr, br, wi, bi, lam):
    xcb = xc.astype(BF16)
    r = jax.nn.sigmoid(_dot(xcb, wr) + br)
    ig = jax.nn.sigmoid(_dot(xcb, wi) + bi)
    nl = -lam
    softplus = jnp.maximum(nl, 0.0) + jnp.log1p(jnp.exp(-jnp.abs(nl)))
    log_a = -LRU_C * r * softplus
    a = jnp.exp(log_a)
    th = jnp.tanh(log_a)
    u = jnp.sqrt(-2.0 * th / (1.0 - th)) * (ig * xc)
    return a, u


def _rglru_kernel(dx_ref, dg_ref, cw_ref, cb_ref, wr_ref, br_ref, wi_ref, bi_ref, lam_ref,
                  y_ref, dtail_ref, hlast_ref, pad_ref, h_ref):
    tb = dx_ref.shape[0]

    @pl.when(pl.program_id(1) == 0)
    def _():
        pad_ref[0:PAD_ROWS, :] = jnp.zeros((PAD_ROWS, D_WIDTH), F32)
        h_ref[...] = jnp.zeros(h_ref.shape, F32)

    x = dx_ref[...]
    pad_ref[PAD_ROWS:PAD_ROWS + tb, :] = x
    cw = cw_ref[...]
    xc = pad_ref[PAD_ROWS - 3:PAD_ROWS - 3 + tb, :] * cw[0:1]
    xc = xc + pad_ref[PAD_ROWS - 2:PAD_ROWS - 2 + tb, :] * cw[1:2]
    xc = xc + pad_ref[PAD_ROWS - 1:PAD_ROWS - 1 + tb, :] * cw[2:3]
    xc = xc + x * cw[3:4]
    xc = xc + cb_ref[...]
    a, u = _lru_coeffs(xc, wr_ref[...], br_ref[...], wi_ref[...], bi_ref[...], lam_ref[...])
    row = lax.broadcasted_iota(jnp.int32, a.shape, 0)
    d = 1
    while d < tb:
        keep = row >= d
        a_prev = jnp.where(keep, pltpu.roll(a, d, 0), 1.0)
        u_prev = jnp.where(keep, pltpu.roll(u, d, 0), 0.0)
        u = a * u_prev + u
        a = a * a_prev
        d *= 2
    h = a * h_ref[0:1, :] + u
    y_ref[...] = (jax.nn.gelu(dg_ref[...]) * h).astype(y_ref.dtype)
    h_ref[0:1, :] = h[tb - 1:tb, :]
    hlast_ref[...] = h[tb - 1:tb, :]
    dtail_ref[...] = pad_ref[PAD_ROWS + tb - (D_CONV - 1):PAD_ROWS + tb, :]
    pad_ref[0:PAD_ROWS, :] = pad_ref[tb:tb + PAD_ROWS, :]


def _rglru_prompt(z3, cw, cb, wr, br, wi, bi, lam):
    bn, s_len, _ = z3.shape
    tb = min(256, s_len)
    nt = s_len // tb
    spec = lambda name: pl.BlockSpec((None, tb, 512), lambda b, t: (b, t, _blk(name, 512)))
    vec = pl.BlockSpec((1, D_WIDTH), lambda b, t: (0, 0))
    mat = pl.BlockSpec((D_WIDTH, D_WIDTH), lambda b, t: (0, 0))
    return pl.pallas_call(
        _rglru_kernel,
        grid=(bn, nt),
        in_specs=[spec("dx"), spec("dg"), pl.BlockSpec((D_CONV, D_WIDTH), lambda b, t: (0, 0)),
                  vec, mat, vec, mat, vec, vec],
        out_specs=[pl.BlockSpec((None, tb, D_WIDTH), lambda b, t: (b, t, 0)),
                   pl.BlockSpec((None, D_CONV - 1, D_WIDTH), lambda b, t: (b, 0, 0)),
                   pl.BlockSpec((None, 1, D_WIDTH), lambda b, t: (b, 0, 0))],
        out_shape=[jax.ShapeDtypeStruct((bn, s_len, D_WIDTH), BF16),
                   jax.ShapeDtypeStruct((bn, D_CONV - 1, D_WIDTH), F32),
                   jax.ShapeDtypeStruct((bn, 1, D_WIDTH), F32)],
        scratch_shapes=[pltpu.VMEM((PAD_ROWS + tb, D_WIDTH), F32),
                        pltpu.VMEM((8, D_WIDTH), F32)],
        compiler_params=_cparams(("parallel", "arbitrary")),
        name="rglru_prompt",
    )(z3, z3, cw, cb, wr, br, wi, bi, lam)


def _sample_bd_kernel(gb_ref, gc_ref, xb_ref, dg_ref, dx_ref, sb_ref, sd_ref, h0_ref,
                      bw_ref, cw_ref, cb_ref, wr_ref, br_ref, wi_ref, bi_ref, lam_ref,
                      yb_ref, yd_ref, bnew_ref, dnew_ref, hnew_ref):
    w = B_WIDTH
    p = gc_ref[...] * xb_ref[...]
    bw = bw_ref[...]
    b0 = sb_ref[:, 0:w]
    b1 = sb_ref[:, w:2 * w]
    y = b0 * bw[0:1] + b1 * bw[1:2] + p * bw[2:3]
    yb_ref[...] = (gb_ref[...] * y).astype(yb_ref.dtype)
    bnew_ref[:, 0:w] = b1
    bnew_ref[:, w:2 * w] = p

    w = D_WIDTH
    x = dx_ref[...]
    cw = cw_ref[...]
    d0 = sd_ref[:, 0:w]
    d1 = sd_ref[:, w:2 * w]
    d2 = sd_ref[:, 2 * w:3 * w]
    xc = d0 * cw[0:1]
    xc = xc + d1 * cw[1:2]
    xc = xc + d2 * cw[2:3]
    xc = xc + x * cw[3:4]
    xc = xc + cb_ref[...]
    a, u = _lru_coeffs(xc, wr_ref[...], br_ref[...], wi_ref[...], bi_ref[...], lam_ref[...])
    h = a * h0_ref[...] + u
    hnew_ref[...] = h
    yd_ref[...] = (jax.nn.gelu(dg_ref[...]) * h).astype(yd_ref.dtype)
    dnew_ref[:, 0:w] = d1
    dnew_ref[:, w:2 * w] = d2
    dnew_ref[:, 2 * w:3 * w] = x


def _sample_bd(z, sb, sd, h0, bw, cw, cb, wr, br, wi, bi, lam):
    m = z.shape[0]
    zspec = lambda name: pl.BlockSpec((m, 512), lambda i: (0, _blk(name, 512)))
    full = lambda a: pl.BlockSpec(a.shape, lambda i: (0,) * a.ndim)
    args = (sb, sd, h0, bw, cw, cb, wr, br, wi, bi, lam)
    out_shapes = [jax.ShapeDtypeStruct((m, B_WIDTH), BF16),
                  jax.ShapeDtypeStruct((m, D_WIDTH), BF16),
                  jax.ShapeDtypeStruct((m, (B_CONV - 1) * B_WIDTH), F32),
                  jax.ShapeDtypeStruct((m, (D_CONV - 1) * D_WIDTH), F32),
                  jax.ShapeDtypeStruct((m, D_WIDTH), F32)]
    return pl.pallas_call(
        _sample_bd_kernel,
        grid=(1,),
        in_specs=[zspec("gb"), zspec("gc"), zspec("xb"), zspec("dg"), zspec("dx")]
                 + [full(a) for a in args],
        out_specs=[pl.BlockSpec(s.shape, lambda i: (0, 0)) for s in out_shapes],
        out_shape=out_shapes,
        compiler_params=_cparams(("arbitrary",)),
        name="sample_bd",
    )(z, z, z, z, z, *args)


def _merge_kernel(ya_ref, yb_ref, yc_ref, yd_ref, g0_ref, g1_ref, g2_ref, g3_ref, x_ref,
                  wa_ref, wb_ref, wc_ref, wd_ref, wo_ref, gp_ref, o_ref):
    acc = jax.nn.sigmoid(g0_ref[...]) * _dot(ya_ref[...], wa_ref[...])
    acc = acc + jax.nn.sigmoid(g1_ref[...]) * _dot(yb_ref[...], wb_ref[...])
    acc = acc + jax.nn.sigmoid(g2_ref[...]) * _dot(yc_ref[...], wc_ref[...])
    acc = acc + jax.nn.sigmoid(g3_ref[...]) * _dot(yd_ref[...], wd_ref[...])
    m = _dot(acc.astype(BF16), wo_ref[...])
    o_ref[...] = x_ref[...] + _rms(m, gp_ref[...])


def _merge(ya, yb, yc, yd, z, x, wa, wb, wc, wd, wo, gp, tm):
    m = x.shape[0]
    tm = min(tm, m)
    row = lambda w: pl.BlockSpec((tm, w), lambda i: (i, 0))
    gate = lambda k: pl.BlockSpec((tm, D_MODEL), lambda i: (i, _blk("gt", D_MODEL) + k))
    const = lambda a: pl.BlockSpec(a.shape, lambda i: (0, 0))
    return pl.pallas_call(
        _merge_kernel,
        grid=(m // tm,),
        in_specs=[row(512), row(512), row(512), row(512), gate(0), gate(1), gate(2), gate(3),
                  row(D_MODEL), const(wa), const(wb), const(wc), const(wd), const(wo), const(gp)],
        out_specs=row(D_MODEL),
        out_shape=jax.ShapeDtypeStruct((m, D_MODEL), F32),
        compiler_params=_cparams(("parallel",)),
        name="merge",
    )(ya, yb, yc, yd, z, z, z, z, x, wa, wb, wc, wd, wo, gp)


def _ffn_kernel(x_ref, gpre_ref, wg_ref, wu_ref, wd_ref, gpost_ref, o_ref, h_ref, acc_ref, *, nf):
    j = pl.program_id(1)

    @pl.when(j == 0)
    def _():
        h_ref[...] = _rms(x_ref[...], gpre_ref[...]).astype(BF16)
        acc_ref[...] = jnp.zeros(acc_ref.shape, F32)

    h = h_ref[...]
    act = jax.nn.silu(_dot(h, wg_ref[...])) * _dot(h, wu_ref[...])
    acc_ref[...] += _dot(act.astype(BF16), wd_ref[...])

    @pl.when(j == nf - 1)
    def _():
        o_ref[...] = x_ref[...] + _rms(acc_ref[...], gpost_ref[...])


def _ffn(x, gpre, wg, wu, wd, gpost, tm, tf=256):
    m = x.shape[0]
    tm = min(tm, m)
    nf = D_FF // tf
    return pl.pallas_call(
        functools.partial(_ffn_kernel, nf=nf),
        grid=(m // tm, nf),
        in_specs=[pl.BlockSpec((tm, D_MODEL), lambda i, j: (i, 0)),
                  pl.BlockSpec((1, D_MODEL), lambda i, j: (0, 0)),
                  pl.BlockSpec((D_MODEL, tf), lambda i, j: (0, j)),
                  pl.BlockSpec((D_MODEL, tf), lambda i, j: (0, j)),
                  pl.BlockSpec((tf, D_MODEL), lambda i, j: (j, 0)),
                  pl.BlockSpec((1, D_MODEL), lambda i, j: (0, 0))],
        out_specs=pl.BlockSpec((tm, D_MODEL), lambda i, j: (i, 0)),
        out_shape=jax.ShapeDtypeStruct((m, D_MODEL), F32),
        scratch_shapes=[pltpu.VMEM((tm, D_MODEL), BF16), pltpu.VMEM((tm, D_MODEL), F32)],
        compiler_params=_cparams(("parallel", "arbitrary")),
        name="ffn",
    )(x, gpre, wg, wu, wd, gpost)


IDX_ROWS = 8
PAGES_PER_SCORE_STEP = 16
PAGES_PER_ATTN_STEP = 8


def _sample_scores_kernel(pt_ref, iq_ref, iw_ref, ikn_ref, *rest, g_pages, n_pages):
    ki_refs = rest[:g_pages]
    o_ref = rest[g_pages]
    j = pl.program_id(1)
    qi = iq_ref[...].astype(BF16)
    wi = iw_ref[...]
    rel = jnp.concatenate([_dot(qi, ki_refs[g][...].astype(BF16)) for g in range(g_pages)], axis=-1)
    sc = jnp.sum(wi * jnp.maximum(rel, 0.0), axis=0, keepdims=True)
    start = pl.multiple_of(j * (g_pages * PAGE_SIZE), g_pages * PAGE_SIZE)
    o_ref[:, pl.ds(start, g_pages * PAGE_SIZE)] = sc

    @pl.when(j == n_pages // g_pages - 1)
    def _():
        kn = ikn_ref[...].astype(BF16).astype(F32)
        rel = jnp.maximum(jnp.sum(qi.astype(F32) * kn, axis=1, keepdims=True), 0.0)
        sc = jnp.sum(wi * rel, axis=0, keepdims=True)
        lane = lax.broadcasted_iota(jnp.int32, (1, PAGE_SIZE), 1)
        o_ref[:, n_pages * PAGE_SIZE:(n_pages + 1) * PAGE_SIZE] = jnp.where(lane == 0, sc, -jnp.inf)


def _sample_scores(pt_flat, iq3, iw3, ikn3, cache_ki, layer, n_pages, g_pages):
    bd = iq3.shape[0]
    ns = (n_pages + 1) * PAGE_SIZE

    def page_map(g):
        return lambda b, j, pt: (layer, pt[b * n_pages + j * g_pages + g], 0, 0)

    grid_spec = pltpu.PrefetchScalarGridSpec(
        num_scalar_prefetch=1,
        grid=(bd, n_pages // g_pages),
        in_specs=[pl.BlockSpec((None, IDX_ROWS, IDX_DIM), lambda b, j, pt: (b, 0, 0)),
                  pl.BlockSpec((None, IDX_ROWS, 1), lambda b, j, pt: (b, 0, 0)),
                  pl.BlockSpec((None, 1, IDX_DIM), lambda b, j, pt: (b, 0, 0))]
                 + [pl.BlockSpec((None, None, IDX_DIM, PAGE_SIZE), page_map(g)) for g in range(g_pages)],
        out_specs=pl.BlockSpec((None, 1, ns), lambda b, j, pt: (b, 0, 0)),
    )
    return pl.pallas_call(
        functools.partial(_sample_scores_kernel, g_pages=g_pages, n_pages=n_pages),
        grid_spec=grid_spec,
        out_shape=jax.ShapeDtypeStruct((bd, 1, ns), F32),
        compiler_params=_cparams(("parallel", "arbitrary")),
        name="sample_scores",
    )(pt_flat, iq3, iw3, ikn3, *([cache_ki] * g_pages))


def _sample_select_kernel(s_ref, o_ref, *, topk, nbits, n_valid):
    xs0 = s_ref[...]
    col = lax.broadcasted_iota(jnp.int32, xs0.shape, 1)
    allowed = col < n_valid
    xs = jnp.where(allowed, xs0 + 0.0, -jnp.inf)
    sel = _topk_select(xs, allowed, col, topk, nbits, N_BISECT_SAMPLE)
    o_ref[...] = jnp.where(sel, 1.0, 0.0)


def _sample_select(scores, topk, n_valid):
    bd, ns = scores.shape
    nbits = max(1, (ns - 1).bit_length())
    return pl.pallas_call(
        functools.partial(_sample_select_kernel, topk=topk, nbits=nbits, n_valid=n_valid),
        grid=(1,),
        in_specs=[pl.BlockSpec((bd, ns), lambda i: (0, 0))],
        out_specs=pl.BlockSpec((bd, ns), lambda i: (0, 0)),
        out_shape=jax.ShapeDtypeStruct((bd, ns), F32),
        compiler_params=_cparams(("arbitrary",)),
        name="sample_select",
    )(scores)


def _online_update(s, weighted_values, m_ref, l_ref, acc_ref):
    m_old = m_ref[...]
    m_new = jnp.maximum(m_old, jnp.max(s, axis=1, keepdims=True))
    alpha = jnp.exp(m_old - m_new)
    p = jnp.exp(s - m_new)
    l_ref[...] = alpha * l_ref[...] + _rowsum(p)
    acc_ref[...] = alpha * acc_ref[...] + weighted_values(p.astype(BF16))
    m_ref[...] = m_new


def _online_update_one(s, v_row, m_ref, l_ref, acc_ref):
    m_old = m_ref[...]
    m_new = jnp.maximum(m_old, s)
    alpha = jnp.exp(m_old - m_new)
    p = jnp.exp(s - m_new)
    l_ref[...] = alpha * l_ref[...] + p
    acc_ref[...] = alpha * acc_ref[...] + p.astype(BF16).astype(F32) * v_row
    m_ref[...] = m_new


def _sample_attn_kernel(pt_ref, mask_ref, qa_ref, ql_ref, qr_ref, kn_ref, vn_ref, cn_ref, rn_ref,
                        *rest, g_pages, n_pages):
    k_refs = rest[0:g_pages]
    v_refs = rest[g_pages:2 * g_pages]
    c_refs = rest[2 * g_pages:3 * g_pages]
    r_refs = rest[3 * g_pages:4 * g_pages]
    oa_ref, oc_ref, ma_ref, la_ref, acca_ref, mc_ref, lc_ref, accc_ref = rest[4 * g_pages:]
    j = pl.program_id(1)

    @pl.when(j == 0)
    def _():
        for m_ref, l_ref, acc_ref in ((ma_ref, la_ref, acca_ref), (mc_ref, lc_ref, accc_ref)):
            m_ref[...] = jnp.full(m_ref.shape, NEG_BIG, F32)
            l_ref[...] = jnp.zeros(l_ref.shape, F32)
            acc_ref[...] = jnp.zeros(acc_ref.shape, F32)

    qa = qa_ref[...].astype(BF16)
    ql = ql_ref[...]
    qr = qr_ref[...].astype(BF16)
    span = g_pages * PAGE_SIZE
    start = pl.multiple_of(j * span, span)
    page = lambda x, g: x[:, g * PAGE_SIZE:(g + 1) * PAGE_SIZE]

    member = mask_ref[:, pl.ds(start, span)] > 0.5
    s = jnp.concatenate([_dot(qa, k_refs[g][...].astype(BF16)) for g in range(g_pages)], axis=-1)
    s = jnp.where(member, s * A_SCALE, -jnp.inf)
    _online_update(s, lambda p: sum(_dot_nt(page(p, g), v_refs[g][...].astype(BF16))
                                    for g in range(g_pages)), ma_ref, la_ref, acca_ref)

    cks = [c_refs[g][...].astype(BF16) for g in range(g_pages)]
    s = jnp.concatenate([_dot_nt(ql, cks[g]) + _dot(qr, r_refs[g][...].astype(BF16))
                         for g in range(g_pages)], axis=-1) * C_SCALE
    _online_update(s, lambda p: sum(_dot(page(p, g), cks[g]) for g in range(g_pages)),
                   mc_ref, lc_ref, accc_ref)

    @pl.when(j == n_pages // g_pages - 1)
    def _():
        rnd = lambda x: x.astype(BF16).astype(F32)
        member = mask_ref[:, n_pages * PAGE_SIZE:n_pages * PAGE_SIZE + 1] > 0.5
        s = jnp.sum(rnd(qa_ref[...]) * rnd(kn_ref[...]), axis=1, keepdims=True) * A_SCALE
        s = jnp.where(member, s, -jnp.inf)
        _online_update_one(s, rnd(vn_ref[...]), ma_ref, la_ref, acca_ref)
        cn = rnd(cn_ref[...])
        s = (jnp.sum(ql.astype(F32) * cn, axis=1, keepdims=True)
             + jnp.sum(rnd(qr_ref[...]) * rnd(rn_ref[...]), axis=1, keepdims=True)) * C_SCALE
        _online_update_one(s, cn, mc_ref, lc_ref, accc_ref)
        o = acca_ref[...] / la_ref[...]
        first_group = lax.broadcasted_iota(jnp.int32, oa_ref.shape, 0) < A_HEADS // A_KV_HEADS
        oa_ref[...] = jnp.where(first_group, o[:, :A_HEAD_DIM], o[:, A_HEAD_DIM:])
        oc_ref[...] = accc_ref[...] / lc_ref[...]


def _sample_attn(pt_flat, mask3, qa, ql, qr, kn, vn, cn, rn, cache_k, cache_v, cache_c, cache_r,
                 layer, n_pages, g_pages):
    bd = qa.shape[0]
    ns = mask3.shape[-1]

    def page_map(g):
        return lambda b, j, pt: (layer, pt[b * n_pages + j * g_pages + g], 0, 0)

    per_b = lambda r, w: pl.BlockSpec((None, r, w), lambda b, j, pt: (b, 0, 0))
    pages = lambda r, w: [pl.BlockSpec((None, None, r, w), page_map(g)) for g in range(g_pages)]
    grid_spec = pltpu.PrefetchScalarGridSpec(
        num_scalar_prefetch=1,
        grid=(bd, n_pages // g_pages),
        in_specs=[per_b(1, ns), per_b(A_HEADS, LANES), per_b(C_HEADS, C_KV_RANK), per_b(C_HEADS, C_ROPE),
                  per_b(1, LANES), per_b(1, LANES), per_b(1, C_KV_RANK), per_b(1, C_ROPE)]
                 + pages(A_KV_HEADS * A_HEAD_DIM, PAGE_SIZE) + pages(A_KV_HEADS * A_HEAD_DIM, PAGE_SIZE)
                 + pages(PAGE_SIZE, C_KV_RANK) + pages(C_ROPE, PAGE_SIZE),
        out_specs=[per_b(A_HEADS, A_HEAD_DIM), per_b(C_HEADS, C_KV_RANK)],
        scratch_shapes=[pltpu.VMEM((A_HEADS, 1), F32), pltpu.VMEM((A_HEADS, 1), F32),
                        pltpu.VMEM((A_HEADS, LANES), F32),
                        pltpu.VMEM((C_HEADS, 1), F32), pltpu.VMEM((C_HEADS, 1), F32),
                        pltpu.VMEM((C_HEADS, C_KV_RANK), F32)],
    )
    return pl.pallas_call(
        functools.partial(_sample_attn_kernel, g_pages=g_pages, n_pages=n_pages),
        grid_spec=grid_spec,
        out_shape=[jax.ShapeDtypeStruct((bd, A_HEADS, A_HEAD_DIM), F32),
                   jax.ShapeDtypeStruct((bd, C_HEADS, C_KV_RANK), F32)],
        compiler_params=_cparams(("parallel", "arbitrary")),
        name="sample_attn",
    )(pt_flat, mask3, qa, ql, qr, kn, vn, cn, rn,
      *([cache_k] * g_pages), *([cache_v] * g_pages), *([cache_c] * g_pages), *([cache_r] * g_pages))


def _uv_kernel(o_ref, wuv_ref, y_ref):
    w = 2 * C_KV_RANK
    outs = [_dot(o_ref[:, w * p:w * (p + 1)].astype(BF16), wuv_ref[p]) for p in range(C_HEADS // 2)]
    y_ref[...] = jnp.concatenate(outs, axis=-1).astype(y_ref.dtype)


def _uv_proj(o_lat, wuvp):
    m = o_lat.shape[0]
    return pl.pallas_call(
        _uv_kernel,
        grid=(1,),
        in_specs=[pl.BlockSpec(o_lat.shape, lambda i: (0, 0)),
                  pl.BlockSpec(wuvp.shape, lambda i: (0, 0, 0))],
        out_specs=pl.BlockSpec((m, C_WIDTH), lambda i: (0, 0)),
        out_shape=jax.ShapeDtypeStruct((m, C_WIDTH), BF16),
        compiler_params=_cparams(("arbitrary",)),
        name="uv_proj",
    )(o_lat, wuvp)


def _rope_tables(pos):
    half = C_ROPE // 2
    freqs = ROPE_THETA ** (-jnp.arange(half, dtype=F32) / half)
    ang = pos.astype(F32)[:, None] * freqs[None, :]
    return jnp.tile(jnp.cos(ang), (1, C_HEADS)), jnp.tile(jnp.sin(ang), (1, C_HEADS))


def _layer_weights(l, w_in, w_out_a, conv_b, w_out_b, g_cq, w_uq, g_ckv, w_uk, w_uv, w_out_c,
                   conv_d_w, conv_d_b, w_rg, b_rg, w_ig, b_ig, lru_lambda, w_out_d, w_o,
                   w_ffn_gate, w_ffn_up, w_ffn_down):
    lw = {}
    lw["w_in"] = _relayout_w_in(w_in[l])
    uq = w_uq[l].reshape(C_Q_RANK, C_HEADS, C_NOPE + C_ROPE)
    half = C_ROPE // 2
    lw["w_uq"] = jnp.concatenate(
        [uq[:, :, :C_NOPE].reshape(C_Q_RANK, -1),
         uq[:, :, C_NOPE:C_NOPE + half].reshape(C_Q_RANK, -1),
         uq[:, :, C_NOPE + half:].reshape(C_Q_RANK, -1)], axis=1).astype(BF16)
    ukt = jnp.transpose(w_uk[l], (1, 2, 0))
    zk = jnp.zeros((C_NOPE, C_KV_RANK), F32)
    lw["w_uk"] = jnp.stack([
        jnp.concatenate([jnp.concatenate([ukt[2 * p], zk], axis=1),
                         jnp.concatenate([zk, ukt[2 * p + 1]], axis=1)], axis=0)
        for p in range(C_HEADS // 2)]).astype(BF16)
    uv = jnp.transpose(w_uv[l], (1, 0, 2))
    zv = jnp.zeros((C_KV_RANK, C_VDIM), F32)
    lw["w_uv"] = jnp.stack([
        jnp.concatenate([jnp.concatenate([uv[2 * p], zv], axis=1),
                         jnp.concatenate([zv, uv[2 * p + 1]], axis=1)], axis=0)
        for p in range(C_HEADS // 2)]).astype(BF16)
    eye = jnp.eye(D_BLOCKS, dtype=F32)
    bd = lambda w: (eye[:, None, :, None] * w[:, :, None, :]).reshape(D_WIDTH, D_WIDTH).astype(BF16)
    lw["w_rg"] = bd(w_rg[l])
    lw["w_ig"] = bd(w_ig[l])
    row = lambda v: v[l][None, :]
    lw["b_rg"], lw["b_ig"], lw["lam"], lw["conv_d_b"] = row(b_rg), row(b_ig), row(lru_lambda), row(conv_d_b)
    lw["g_cq"], lw["g_ckv"] = row(g_cq), row(g_ckv)
    lw["conv_b"], lw["conv_d_w"] = conv_b[l], conv_d_w[l]
    for name, w in (("w_out_a", w_out_a), ("w_out_b", w_out_b), ("w_out_c", w_out_c),
                    ("w_out_d", w_out_d), ("w_o", w_o), ("w_ffn_gate", w_ffn_gate),
                    ("w_ffn_up", w_ffn_up), ("w_ffn_down", w_ffn_down)):
        lw[name] = w[l].astype(BF16)
    return lw


def _zcols(z, name, lo, hi):
    o = _DST[name][0]
    return z[..., o + lo:o + hi]


def _state_from_z(z):
    hd = A_HEAD_DIM
    ak = jnp.concatenate([_zcols(z, "ak", 0, hd), _zcols(z, "ak", 2 * hd, 3 * hd)], axis=-1)
    av = jnp.concatenate([_zcols(z, "av", 0, hd), _zcols(z, "av", 2 * hd, 3 * hd)], axis=-1)
    ik = _zcols(z, "ik", 0, IDX_DIM)
    return ak, av, ik


def _krope_state(kr):
    half = C_ROPE // 2
    return jnp.concatenate([kr[:, :half], kr[:, LANES:LANES + half]], axis=-1)


def _prompt_layer(x, lw, gains, bn, s_len, cos, sin):
    g_mix_pre, g_mix_post, g_ffn_pre, g_ffn_post = gains
    m = bn * s_len
    z = _inproj(x, g_mix_pre, lw["w_in"], tm=512)
    z3 = z.reshape(bn, s_len, N_PROJ)
    tm_c = min(512, s_len)
    nb = s_len // tm_c
    q, kcat, ckvn, kr, _ = _cprep(z, cos, sin, lambda i: (i % nb, 0), lw["g_cq"], lw["g_ckv"],
                                  lw["w_uq"], lw["w_uk"], tm=tm_c)
    yc = _mla_prompt(q, kcat, lw["w_uv"], bn, s_len)
    ya = _sparse_prompt(z3, min(TOPK_MAX, s_len // 4)).reshape(m, A_WIDTH)
    yb, b_new = _bconv_prompt(z3, lw["conv_b"])
    yd, d_new, h_new = _rglru_prompt(z3, lw["conv_d_w"], lw["conv_d_b"], lw["w_rg"], lw["b_rg"],
                                     lw["w_ig"], lw["b_ig"], lw["lam"])
    x = _merge(ya, yb.reshape(m, B_WIDTH), yc, yd.reshape(m, D_WIDTH), z, x,
               lw["w_out_a"], lw["w_out_b"], lw["w_out_c"], lw["w_out_d"], lw["w_o"], g_mix_post, tm=512)
    x = _ffn(x, g_ffn_pre, lw["w_ffn_gate"], lw["w_ffn_up"], lw["w_ffn_down"], g_ffn_post, tm=512)
    ak, av, ik = _state_from_z(z3)
    st = (ak.reshape(bn, s_len, A_KV_HEADS, A_HEAD_DIM), av.reshape(bn, s_len, A_KV_HEADS, A_HEAD_DIM),
          ik, ckvn.reshape(bn, s_len, C_KV_RANK), _krope_state(kr).reshape(bn, s_len, C_ROPE),
          b_new, d_new, h_new.reshape(bn, D_WIDTH))
    return x, st


def _sample_layer(x, lw, gains, layer, caches, states, pt_flat, n_pages, cos, sin, g_pages):
    g_mix_pre, g_mix_post, g_ffn_pre, g_ffn_post = gains
    cache_k, cache_v, cache_ki, cache_c, cache_r = caches
    sb, sd, h0 = states
    bd = x.shape[0]
    z = _inproj(x, g_mix_pre, lw["w_in"], tm=bd)
    q, _, ckvn, kr, qrope = _cprep(z, cos, sin, lambda i: (0, 0), lw["g_cq"], lw["g_ckv"],
                                   lw["w_uq"], lw["w_uk"], tm=bd)
    ak, av, ik = _state_from_z(z)
    krope = _krope_state(kr)

    iq3 = jnp.pad(_zcols(z, "iq", 0, 256).reshape(bd, IDX_HEADS, IDX_DIM),
                  ((0, 0), (0, IDX_ROWS - IDX_HEADS), (0, 0)))
    iw3 = jnp.pad(_zcols(z, "iw", 0, IDX_HEADS), ((0, 0), (0, IDX_ROWS - IDX_HEADS)))[:, :, None]
    scores = _sample_scores(pt_flat, iq3, iw3, ik[:, None, :], cache_ki, layer, n_pages, g_pages[0])
    total = n_pages * PAGE_SIZE + 1
    mask = _sample_select(scores.reshape(bd, -1), min(TOPK_MAX, total // 4), total)

    aq = _zcols(z, "aq", 0, A_WIDTH).reshape(bd, A_KV_HEADS, A_HEADS // A_KV_HEADS, A_HEAD_DIM)
    zq = jnp.zeros_like(aq[:, 0])
    qa = jnp.concatenate([jnp.concatenate([aq[:, 0], zq], axis=-1),
                          jnp.concatenate([zq, aq[:, 1]], axis=-1)], axis=1)
    ql = jnp.transpose(q[:, :, :C_KV_RANK], (1, 0, 2))
    half = C_ROPE // 2
    qr = jnp.concatenate([qrope[:, :LANES].reshape(bd, C_HEADS, half),
                          qrope[:, LANES:].reshape(bd, C_HEADS, half)], axis=-1)
    oa, oc = _sample_attn(pt_flat, mask[:, None, :], qa, ql, qr, ak[:, None, :], av[:, None, :],
                          ckvn[:, None, :], krope[:, None, :], cache_k, cache_v, cache_c, cache_r,
                          layer, n_pages, g_pages[1])
    ya = oa.reshape(bd, A_WIDTH).astype(BF16)
    yc = _uv_proj(oc.reshape(bd, C_HEADS * C_KV_RANK), lw["w_uv"])

    yb, yd, b_new, d_new, h_new = _sample_bd(
        z, sb.reshape(bd, -1), sd.reshape(bd, -1), h0, lw["conv_b"], lw["conv_d_w"], lw["conv_d_b"],
        lw["w_rg"], lw["b_rg"], lw["w_ig"], lw["b_ig"], lw["lam"])
    x = _merge(ya, yb, yc, yd, z, x, lw["w_out_a"], lw["w_out_b"], lw["w_out_c"], lw["w_out_d"],
               lw["w_o"], g_mix_post, tm=bd)
    x = _ffn(x, g_ffn_pre, lw["w_ffn_gate"], lw["w_ffn_up"], lw["w_ffn_down"], g_ffn_post, tm=bd)
    st = (ak.reshape(bd, 1, A_KV_HEADS, A_HEAD_DIM), av.reshape(bd, 1, A_KV_HEADS, A_HEAD_DIM),
          ik[:, None, :], ckvn[:, None, :], krope[:, None, :],
          b_new.reshape(bd, B_CONV - 1, B_WIDTH), d_new.reshape(bd, D_CONV - 1, D_WIDTH), h_new)
    return x, st


def kernel(x_prompt, x_sample, cache_a_k, cache_a_v, cache_a_kidx, cache_c_kv, cache_c_krope, state_b_conv, state_d_conv, state_d_h, page_table, g_mix_pre, g_mix_post, g_ffn_pre, g_ffn_post, w_in, w_out_a, conv_b, w_out_b, g_cq, w_uq, g_ckv, w_uk, w_uv, w_out_c, conv_d_w, conv_d_b, w_rg, b_rg, w_ig, b_ig, lru_lambda, w_out_d, w_o, w_ffn_gate, w_ffn_up, w_ffn_down):
    bn, s_len, _ = x_prompt.shape
    bd, t_len, _ = x_sample.shape
    assert t_len == 1 and s_len % Q_BLOCK == 0
    depth = w_in.shape[0]
    n_pages = page_table.shape[1]
    n_pool = cache_a_k.shape[1]
    past = n_pages * PAGE_SIZE
    largest_group = lambda cap: max(g for g in range(1, cap + 1) if n_pages % g == 0)
    g_pages = (largest_group(PAGES_PER_SCORE_STEP), largest_group(PAGES_PER_ATTN_STEP))

    cos_p, sin_p = _rope_tables(jnp.arange(s_len))
    cos_s, sin_s = _rope_tables(jnp.full((bd,), past))
    pt_flat = page_table.reshape(-1)
    kv_t = lambda c: jnp.transpose(c, (0, 1, 3, 4, 2)).reshape(
        depth, n_pool, A_KV_HEADS * A_HEAD_DIM, PAGE_SIZE)
    caches = (kv_t(cache_a_k), kv_t(cache_a_v), jnp.transpose(cache_a_kidx, (0, 1, 3, 2)),
              cache_c_kv, jnp.transpose(cache_c_krope, (0, 1, 3, 2)))

    xp = x_prompt.reshape(bn * s_len, D_MODEL)
    xs = x_sample.reshape(bd, D_MODEL)
    p_states, s_states = [], []
    for l in range(depth):
        lw = _layer_weights(l, w_in, w_out_a, conv_b, w_out_b, g_cq, w_uq, g_ckv, w_uk, w_uv, w_out_c,
                            conv_d_w, conv_d_b, w_rg, b_rg, w_ig, b_ig, lru_lambda, w_out_d, w_o,
                            w_ffn_gate, w_ffn_up, w_ffn_down)
        gains = tuple(g[l][None, :] for g in (g_mix_pre, g_mix_post, g_ffn_pre, g_ffn_post))
        xp, st_p = _prompt_layer(xp, lw, gains, bn, s_len, cos_p, sin_p)
        xs, st_s = _sample_layer(xs, lw, gains, l, caches,
                                 (state_b_conv[l], state_d_conv[l], state_d_h[l]),
                                 pt_flat, n_pages, cos_s, sin_s, g_pages)
        p_states.append(st_p)
        s_states.append(st_s)
    p_out = [jnp.stack([st[j] for st in p_states]) for j in range(8)]
    s_out = [jnp.stack([st[j] for st in s_states]) for j in range(8)]
    return (xp.reshape(bn, s_len, D_MODEL), xs.reshape(bd, 1, D_MODEL), *p_out, *s_out)
```

```python
import functools

import numpy as np
import jax
import jax.numpy as jnp
from jax import lax
from jax.experimental import pallas as pl
from jax.experimental.pallas import tpu as pltpu

D_MODEL = 1024
PAGE_SIZE = 128
A_HEADS = 8
A_KV_HEADS = 2
A_HEAD_DIM = 64
A_WIDTH = A_HEADS * A_HEAD_DIM
A_SCALE = A_HEAD_DIM ** -0.5
IDX_HEADS = 4
IDX_DIM = 64
TOPK_MAX = 256
B_WIDTH = 512
B_CONV = 3
C_HEADS = 8
C_NOPE = 64
C_ROPE = 32
C_VDIM = 64
C_Q_RANK = 256
C_KV_RANK = 256
C_WIDTH = C_HEADS * C_VDIM
C_SCALE = (C_NOPE + C_ROPE) ** -0.5
ROPE_THETA = 10000.0
D_WIDTH = 512
D_BLOCKS = 8
D_BLOCK = D_WIDTH // D_BLOCKS
D_CONV = 4
LRU_C = 8.0
N_BRANCH = 4
D_FF = ((8 * D_MODEL + 3 * 256 - 1) // (3 * 256)) * 256
Q_BLOCK = 128
EPS = 1e-6

LANES = 128
VMEM_LIMIT = 56 * 1024 * 1024
NEG_BIG = -1e30

BF16 = jnp.bfloat16
F32 = jnp.float32

_SRC = {}
_o = 0
for _n, _w in (("aq", A_WIDTH), ("ak", 128), ("av", 128), ("iq", 256), ("ik", 64), ("iw", 4),
               ("gb", 512), ("gc", 512), ("xb", 512), ("cq", 256), ("ckv", 256), ("ckr", 32),
               ("dg", 512), ("dx", 512), ("gt", 4096)):
    _SRC[_n] = (_o, _w)
    _o += _w
N_IN = _o

_DST = {}
_o = 0
for _n, _w in (("aq", 512), ("gb", 512), ("gc", 512), ("xb", 512), ("dg", 512), ("dx", 512),
               ("gt", 4096), ("iq", 256), ("cq", 256), ("ckv", 256), ("ckr", 256),
               ("ak", 256), ("av", 256), ("ik", 128), ("iw", 128)):
    _DST[_n] = (_o, _w)
    _o += _w
N_PROJ = _o
PROJ_TN = 1280


def _blk(name, width):
    off = _DST[name][0]
    assert off % width == 0
    return off // width


def _cparams(sem, vmem=VMEM_LIMIT):
    return pltpu.CompilerParams(dimension_semantics=sem, vmem_limit_bytes=vmem)


def _relayout_w_in(w):
    def src(name, lo=0, hi=None):
        o, wd = _SRC[name]
        hi = wd if hi is None else hi
        return w[:, o + lo:o + hi]

    zeros = lambda n: jnp.zeros((w.shape[0], n), w.dtype)
    parts = [src("aq"), src("gb"), src("gc"), src("xb"), src("dg"), src("dx"), src("gt"),
             src("iq"), src("cq"), src("ckv"),
             jnp.tile(src("ckr", 0, 16), (1, 8)), jnp.tile(src("ckr", 16, 32), (1, 8)),
             src("ak", 0, 64), src("ak", 0, 64), src("ak", 64, 128), src("ak", 64, 128),
             src("av", 0, 64), src("av", 0, 64), src("av", 64, 128), src("av", 64, 128),
             src("ik"), src("ik"),
             src("iw"), zeros(124)]
    out = jnp.concatenate(parts, axis=1)
    assert out.shape[1] == N_PROJ
    return out.astype(BF16)


def _rms(x, g):
    inv = lax.rsqrt(jnp.mean(x * x, axis=-1, keepdims=True) + EPS)
    return (x * inv) * g


def _dot(a, b):
    return jnp.dot(a, b, preferred_element_type=F32)


def _dot_nt(a, b):
    return lax.dot_general(a, b, (((1,), (1,)), ((), ())), preferred_element_type=F32)


def _inproj_kernel(x_ref, g_ref, w_ref, o_ref, h_ref):
    @pl.when(pl.program_id(1) == 0)
    def _():
        h_ref[...] = _rms(x_ref[...], g_ref[...]).astype(BF16)

    o_ref[...] = _dot(h_ref[...], w_ref[...])


def _inproj(x, g, w, tm):
    m = x.shape[0]
    tm = min(tm, m)
    return pl.pallas_call(
        _inproj_kernel,
        grid=(m // tm, N_PROJ // PROJ_TN),
        in_specs=[pl.BlockSpec((tm, D_MODEL), lambda i, j: (i, 0)),
                  pl.BlockSpec((1, D_MODEL), lambda i, j: (0, 0)),
                  pl.BlockSpec((D_MODEL, PROJ_TN), lambda i, j: (0, j))],
        out_specs=pl.BlockSpec((tm, PROJ_TN), lambda i, j: (i, j)),
        out_shape=jax.ShapeDtypeStruct((m, N_PROJ), F32),
        scratch_shapes=[pltpu.VMEM((tm, D_MODEL), BF16)],
        compiler_params=_cparams(("parallel", "arbitrary")),
        name="in_proj",
    )(x, g, w)


def _cprep_kernel(cq_ref, ckv_ref, ckr_ref, cos_ref, sin_ref, gq_ref, gkv_ref, wuq_ref, wuk_ref,
                  q_ref, kcat_ref, ckvn_ref, kr_ref, qr_ref):
    cos = cos_ref[...]
    sin = sin_ref[...]
    cqn = _rms(cq_ref[...], gq_ref[...]).astype(BF16)
    q = _dot(cqn, wuq_ref[...])
    nope = C_HEADS * C_NOPE
    r1 = q[:, nope:nope + LANES]
    r2 = q[:, nope + LANES:nope + 2 * LANES]
    o1 = r1 * cos - r2 * sin
    o2 = r1 * sin + r2 * cos
    qr_ref[...] = jnp.concatenate([o1, o2], axis=-1)
    head_of_lane = lax.broadcasted_iota(jnp.int32, o1.shape, 1) // (C_ROPE // 2)
    for p in range(C_HEADS // 2):
        ql = _dot(q[:, LANES * p:LANES * (p + 1)].astype(BF16), wuk_ref[p])
        for e in range(2):
            h = 2 * p + e
            sel = head_of_lane == h
            q_ref[h] = jnp.concatenate(
                [ql[:, C_KV_RANK * e:C_KV_RANK * (e + 1)],
                 jnp.where(sel, o1, 0.0), jnp.where(sel, o2, 0.0)], axis=-1).astype(BF16)
    ckvn = _rms(ckv_ref[...], gkv_ref[...])
    ckvn_ref[...] = ckvn
    c = ckr_ref[...]
    c1 = c[:, :LANES]
    c2 = c[:, LANES:]
    k1 = c1 * cos - c2 * sin
    k2 = c1 * sin + c2 * cos
    kr_ref[...] = jnp.concatenate([k1, k2], axis=-1)
    kcat_ref[...] = jnp.concatenate([ckvn, k1, k2], axis=-1).astype(BF16)


def _cprep(z, cos, sin, tab_map, g_cq, g_ckv, wuq, wukp, tm):
    m = z.shape[0]
    tm = min(tm, m)
    qw = C_KV_RANK + 2 * LANES
    const2 = lambda i: (0, 0)
    return pl.pallas_call(
        _cprep_kernel,
        grid=(m // tm,),
        in_specs=[pl.BlockSpec((tm, 256), lambda i: (i, _blk("cq", 256))),
                  pl.BlockSpec((tm, 256), lambda i: (i, _blk("ckv", 256))),
                  pl.BlockSpec((tm, 256), lambda i: (i, _blk("ckr", 256))),
                  pl.BlockSpec((tm, LANES), tab_map),
                  pl.BlockSpec((tm, LANES), tab_map),
                  pl.BlockSpec((1, C_Q_RANK), const2),
                  pl.BlockSpec((1, C_KV_RANK), const2),
                  pl.BlockSpec(wuq.shape, const2),
                  pl.BlockSpec(wukp.shape, lambda i: (0, 0, 0))],
        out_specs=[pl.BlockSpec((C_HEADS, tm, qw), lambda i: (0, i, 0)),
                   pl.BlockSpec((tm, qw), lambda i: (i, 0)),
                   pl.BlockSpec((tm, C_KV_RANK), lambda i: (i, 0)),
                   pl.BlockSpec((tm, 2 * LANES), lambda i: (i, 0)),
                   pl.BlockSpec((tm, 2 * LANES), lambda i: (i, 0))],
        out_shape=[jax.ShapeDtypeStruct((C_HEADS, m, qw), BF16),
                   jax.ShapeDtypeStruct((m, qw), BF16),
                   jax.ShapeDtypeStruct((m, C_KV_RANK), F32),
                   jax.ShapeDtypeStruct((m, 2 * LANES), F32),
                   jax.ShapeDtypeStruct((m, 2 * LANES), F32)],
        compiler_params=_cparams(("parallel",)),
        name="c_prep",
    )(z, z, z, cos, sin, g_cq, g_ckv, wuq, wukp)


MLA_CHUNK_HEADS = 2
MLA_KEY_BLOCK = 512


def _mla_kernel(q_ref, k_ref, wuv_ref, o_ref, m_ref, l_ref, acc_ref, *, tk, nkb):
    i = pl.program_id(1)
    j = pl.program_id(2)
    rows = C_HEADS * Q_BLOCK

    @pl.when(j == 0)
    def _():
        m_ref[...] = jnp.full(m_ref.shape, NEG_BIG, F32)
        l_ref[...] = jnp.zeros(l_ref.shape, F32)
        acc_ref[...] = jnp.zeros(acc_ref.shape, F32)

    def step(masked):
        k = k_ref[...]
        v = k[:, :C_KV_RANK]
        n_chunk = C_HEADS // MLA_CHUNK_HEADS
        chunk_rows = MLA_CHUNK_HEADS * Q_BLOCK

        def scores(c):
            q = q_ref[c * MLA_CHUNK_HEADS:(c + 1) * MLA_CHUNK_HEADS].reshape(chunk_rows, q_ref.shape[-1])
            return _dot_nt(q, k) * C_SCALE

        lane_tiles = lambda x: [x[:, LANES * u:LANES * (u + 1)] for u in range(x.shape[1] // LANES)]
        s_next = scores(0)
        for c in range(n_chunk):
            r0, r1 = c * chunk_rows, (c + 1) * chunk_rows
            s = s_next
            if c + 1 < n_chunk:
                s_next = scores(c + 1)
            if masked:
                t = i * Q_BLOCK + lax.broadcasted_iota(jnp.int32, s.shape, 0) % Q_BLOCK
                col = j * tk + lax.broadcasted_iota(jnp.int32, s.shape, 1)
                s = jnp.where(col <= t, s, -jnp.inf)
            m_old = m_ref[r0:r1, :]
            m_new = jnp.maximum(m_old, jnp.max(s, axis=1, keepdims=True))
            alpha = jnp.exp(m_old - m_new)
            p_tiles = [jnp.exp(st - m_new) for st in lane_tiles(s)]
            l_ref[r0:r1, :] = alpha * l_ref[r0:r1, :] + sum(p_tiles)
            pv = _dot(jnp.concatenate(p_tiles, axis=-1).astype(BF16), v)
            alpha_wide = jnp.concatenate([alpha] * (C_KV_RANK // LANES), axis=-1)
            acc_ref[r0:r1, :] = alpha_wide * acc_ref[r0:r1, :] + pv
            m_ref[r0:r1, :] = m_new

    first_row = i * Q_BLOCK
    fully_visible = j * tk + tk - 1 <= first_row
    pl.when(fully_visible)(functools.partial(step, False))
    pl.when(jnp.logical_and(jnp.logical_not(fully_visible), j * tk <= first_row + Q_BLOCK - 1))(
        functools.partial(step, True))

    @pl.when(j == nkb - 1)
    def _():
        o = acc_ref[...] / jnp.sum(l_ref[...], axis=1, keepdims=True)
        outs = []
        for p in range(C_HEADS // 2):
            pair = jnp.concatenate([o[2 * p * Q_BLOCK:(2 * p + 1) * Q_BLOCK],
                                    o[(2 * p + 1) * Q_BLOCK:(2 * p + 2) * Q_BLOCK]], axis=-1)
            outs.append(_dot(pair.astype(BF16), wuv_ref[p]))
        o_ref[...] = jnp.concatenate(outs, axis=-1).astype(o_ref.dtype)


def _mla_prompt(q, kcat, wuvp, bn, s_len):
    nq = s_len // Q_BLOCK
    tk = min(MLA_KEY_BLOCK, s_len)
    nkb = s_len // tk
    qw = q.shape[-1]

    def k_map(b, i, j):
        last = (i * Q_BLOCK + Q_BLOCK - 1) // tk
        return (b * nkb + jnp.minimum(j, last), 0)

    return pl.pallas_call(
        functools.partial(_mla_kernel, tk=tk, nkb=nkb),
        grid=(bn, nq, nkb),
        in_specs=[pl.BlockSpec((C_HEADS, Q_BLOCK, qw), lambda b, i, j: (0, b * nq + i, 0)),
                  pl.BlockSpec((tk, qw), k_map),
                  pl.BlockSpec(wuvp.shape, lambda b, i, j: (0, 0, 0))],
        out_specs=pl.BlockSpec((Q_BLOCK, C_WIDTH), lambda b, i, j: (b * nq + i, 0)),
        out_shape=jax.ShapeDtypeStruct((bn * s_len, C_WIDTH), BF16),
        scratch_shapes=[pltpu.VMEM((C_HEADS * Q_BLOCK, LANES), F32),
                        pltpu.VMEM((C_HEADS * Q_BLOCK, LANES), F32),
                        pltpu.VMEM((C_HEADS * Q_BLOCK, C_KV_RANK), F32)],
        compiler_params=_cparams(("parallel", "parallel", "arbitrary")),
        name="mla_prompt",
    )(q, kcat, wuvp)


def _rowsum(x):
    return jnp.sum(x, axis=1, keepdims=True)


def _topk_select(xs, allowed, col, k, nbits, n_bisect):
    kf = float(k)
    inf = jnp.float32(jnp.inf)
    row_min = jnp.min(jnp.where(allowed, xs, inf), axis=1, keepdims=True)
    row_max = jnp.max(xs, axis=1, keepdims=True)
    n_allowed = _rowsum(jnp.where(allowed, 1.0, 0.0))

    def bisect(_, c):
        low, high, has_low = c
        mid = low + (high - low) * 0.5
        up = _rowsum(jnp.where(xs > mid, 1.0, 0.0)) >= kf
        return jnp.where(up, mid, low), jnp.where(up, high, mid), jnp.where(up, 1.0, has_low)

    low, high, has_low = lax.fori_loop(0, n_bisect, bisect, (row_min, row_max, jnp.zeros_like(row_min)))
    lo0 = jnp.min(jnp.where(xs > low, xs, inf), axis=1, keepdims=True)
    lo0 = jnp.where((has_low > 0.5) & (n_allowed > kf), lo0, row_min)
    ub0 = jnp.max(jnp.where(xs > high, -inf, xs), axis=1, keepdims=True)
    ub0 = jnp.where(n_allowed > kf, ub0, row_min)

    def cond(c):
        lo, ub = c
        return jnp.max(jnp.where(lo < ub, 1.0, 0.0)) > 0.0

    def body(c):
        lo, ub = c
        mid = lo + (ub - lo) * 0.5
        mid = jnp.where((mid >= lo) & (mid < ub), mid, lo)
        gt = xs > mid
        cnt = _rowsum(jnp.where(gt, 1.0, 0.0))
        vmin = jnp.min(jnp.where(gt, xs, inf), axis=1, keepdims=True)
        vmax = jnp.max(jnp.where(gt, -inf, xs), axis=1, keepdims=True)
        active = lo < ub
        up = cnt >= kf
        lo = jnp.where(active & up, vmin, lo)
        ub = jnp.where(active & jnp.logical_not(up), vmax, ub)
        return lo, ub

    v, _ = lax.while_loop(cond, body, (lo0, ub0))
    gtv = xs > v
    need = kf - _rowsum(jnp.where(gtv, 1.0, 0.0))
    tie = allowed & (xs == v)
    n_tie = _rowsum(jnp.where(tie, 1.0, 0.0))

    def search(_):
        def step(it, c):
            cand = c + jnp.left_shift(jnp.int32(1), nbits - 1 - it)
            f = _rowsum(jnp.where(tie & (col < cand), 1.0, 0.0))
            return jnp.where(f < need, cand, c)
        return lax.fori_loop(0, nbits, step, jnp.zeros(v.shape, jnp.int32))

    def no_search(_):
        return jnp.full(v.shape, (1 << nbits) - 1, jnp.int32)

    any_over = jnp.max(jnp.where(n_tie > need, 1.0, 0.0)) > 0.0
    cut = lax.cond(any_over, search, no_search, 0)
    return gtv | (tie & (col <= cut))


KEY_CLASS = 256
N_BISECT_PROMPT = 14
N_BISECT_SAMPLE = 16


def _sparse_prompt_kernel(q_ref, iq_ref, iw_ref, kd_ref, vd_ref, ikd_ref, o_ref, kb_ref, vb_ref, ib_ref,
                          *, topk, s_len):
    i = pl.program_id(1)

    @pl.when(i == 0)
    def _():
        ib_ref[...] = ikd_ref[...].astype(BF16)
        for g in range(A_KV_HEADS):
            kb_ref[g] = kd_ref[:, LANES * g:LANES * (g + 1)].astype(BF16)
            vb_ref[g] = vd_ref[:, LANES * g:LANES * (g + 1)].astype(BF16)

    first_half = lax.broadcasted_iota(jnp.int32, (Q_BLOCK, LANES), 1) < (LANES // 2)

    def half(x, pair, e):
        blk = x[:, LANES * pair:LANES * (pair + 1)]
        return jnp.where(first_half if e == 0 else jnp.logical_not(first_half), blk, 0.0).astype(BF16)

    def body(width):
        q = q_ref[...] * A_SCALE
        iq = iq_ref[...]
        iw = iw_ref[...]
        ikd = ib_ref[0:width, :]
        score = None
        for h in range(IDX_HEADS):
            rel = jnp.maximum(_dot_nt(half(iq, h // 2, h % 2), ikd), 0.0)
            term = iw[:, h:h + 1] * rel
            score = term if score is None else score + term
        t = i * Q_BLOCK + lax.broadcasted_iota(jnp.int32, score.shape, 0)
        col = lax.broadcasted_iota(jnp.int32, score.shape, 1)
        allowed = col <= t
        xs = jnp.where(allowed, score + 0.0, -jnp.inf)
        nbits = max(1, (width - 1).bit_length())
        sel = _topk_select(xs, allowed, col, topk, nbits, N_BISECT_PROMPT)
        bias = jnp.where(sel, 0.0, -jnp.inf)

        rep = A_HEADS // A_KV_HEADS
        outs = []
        for g in range(A_KV_HEADS):
            kd = kb_ref[g, 0:width, :]
            vd = vb_ref[g, 0:width, :]
            heads = [g * rep + r for r in range(rep)]
            s_all = _dot_nt(jnp.concatenate([half(q, h // 2, h % 2) for h in heads], axis=0), kd)
            ps, ls = [], []
            for r in range(rep):
                s = s_all[r * Q_BLOCK:(r + 1) * Q_BLOCK] + bias
                m = jnp.max(s, axis=1, keepdims=True)
                p = jnp.exp(s - m)
                ls.append(_rowsum(p))
                ps.append(p.astype(BF16))
            o_all = _dot(jnp.concatenate(ps, axis=0), vd)
            for r in range(rep):
                outs.append(o_all[r * Q_BLOCK:(r + 1) * Q_BLOCK] / ls[r])
        pairs = [jnp.where(first_half, outs[2 * j], outs[2 * j + 1]) for j in range(A_HEADS // 2)]
        o_ref[...] = jnp.concatenate(pairs, axis=-1).astype(o_ref.dtype)

    per_class = KEY_CLASS // Q_BLOCK
    n_class = max(1, s_len // KEY_CLASS)
    for c in range(n_class):
        width = min(s_len, (c + 1) * KEY_CLASS)
        pl.when(i // per_class == c)(functools.partial(body, width))


def _sparse_prompt(z3, topk):
    bn, s_len, _ = z3.shape
    nq = s_len // Q_BLOCK
    return pl.pallas_call(
        functools.partial(_sparse_prompt_kernel, topk=topk, s_len=s_len),
        grid=(bn, nq),
        in_specs=[pl.BlockSpec((None, Q_BLOCK, 512), lambda b, i: (b, i, _blk("aq", 512))),
                  pl.BlockSpec((None, Q_BLOCK, 256), lambda b, i: (b, i, _blk("iq", 256))),
                  pl.BlockSpec((None, Q_BLOCK, 128), lambda b, i: (b, i, _blk("iw", 128))),
                  pl.BlockSpec((None, s_len, 256), lambda b, i: (b, 0, _blk("ak", 256))),
                  pl.BlockSpec((None, s_len, 256), lambda b, i: (b, 0, _blk("av", 256))),
                  pl.BlockSpec((None, s_len, 128), lambda b, i: (b, 0, _blk("ik", 128)))],
        out_specs=pl.BlockSpec((None, Q_BLOCK, A_WIDTH), lambda b, i: (b, i, 0)),
        out_shape=jax.ShapeDtypeStruct((bn, s_len, A_WIDTH), BF16),
        scratch_shapes=[pltpu.VMEM((A_KV_HEADS, s_len, LANES), BF16),
                        pltpu.VMEM((A_KV_HEADS, s_len, LANES), BF16),
                        pltpu.VMEM((s_len, LANES), BF16)],
        compiler_params=_cparams(("parallel", "arbitrary")),
        name="sparse_prompt",
    )(z3, z3, z3, z3, z3, z3)


PAD_ROWS = 8


def _bconv_kernel(gb_ref, gc_ref, xb_ref, w_ref, y_ref, tail_ref, pad_ref):
    tb = gb_ref.shape[0]

    @pl.when(pl.program_id(1) == 0)
    def _():
        pad_ref[0:PAD_ROWS, :] = jnp.zeros((PAD_ROWS, B_WIDTH), F32)

    p = gc_ref[...] * xb_ref[...]
    pad_ref[PAD_ROWS:PAD_ROWS + tb, :] = p
    w = w_ref[...]
    y = (pad_ref[PAD_ROWS - 2:PAD_ROWS - 2 + tb, :] * w[0:1]
         + pad_ref[PAD_ROWS - 1:PAD_ROWS - 1 + tb, :] * w[1:2]
         + p * w[2:3])
    y_ref[...] = (gb_ref[...] * y).astype(y_ref.dtype)
    tail_ref[...] = pad_ref[PAD_ROWS + tb - (B_CONV - 1):PAD_ROWS + tb, :]
    pad_ref[0:PAD_ROWS, :] = pad_ref[tb:tb + PAD_ROWS, :]


def _bconv_prompt(z3, conv_w):
    bn, s_len, _ = z3.shape
    tb = min(256, s_len)
    nt = s_len // tb
    spec = lambda name: pl.BlockSpec((None, tb, 512), lambda b, t: (b, t, _blk(name, 512)))
    return pl.pallas_call(
        _bconv_kernel,
        grid=(bn, nt),
        in_specs=[spec("gb"), spec("gc"), spec("xb"),
                  pl.BlockSpec((B_CONV, B_WIDTH), lambda b, t: (0, 0))],
        out_specs=[pl.BlockSpec((None, tb, B_WIDTH), lambda b, t: (b, t, 0)),
                   pl.BlockSpec((None, B_CONV - 1, B_WIDTH), lambda b, t: (b, 0, 0))],
        out_shape=[jax.ShapeDtypeStruct((bn, s_len, B_WIDTH), BF16),
                   jax.ShapeDtypeStruct((bn, B_CONV - 1, B_WIDTH), F32)],
        scratch_shapes=[pltpu.VMEM((PAD_ROWS + tb, B_WIDTH), F32)],
        compiler_params=_cparams(("parallel", "arbitrary")),
        name="bconv_prompt",
    )(z3, z3, z3, conv_w)


def _lru_coeffs(xc, wr, br, wi, bi, lam):
    xcb = xc.astype(BF16)
    r = jax.nn.sigmoid(_dot(xcb, wr) + br)
    ig = jax.nn.sigmoid(_dot(xcb, wi) + bi)
    nl = -lam
    softplus = jnp.maximum(nl, 0.0) + jnp.log1p(jnp.exp(-jnp.abs(nl)))
    log_a = -LRU_C * r * softplus
    a = jnp.exp(log_a)
    th = jnp.tanh(log_a)
    u = jnp.sqrt(-2.0 * th / (1.0 - th)) * (ig * xc)
    return a, u


def _rglru_kernel(dx_ref, dg_ref, cw_ref, cb_ref, wr_ref, br_ref, wi_ref, bi_ref, lam_ref,
                  y_ref, dtail_ref, hlast_ref, pad_ref, h_ref):
    tb = dx_ref.shape[0]

    @pl.when(pl.program_id(1) == 0)
    def _():
        pad_ref[0:PAD_ROWS, :] = jnp.zeros((PAD_ROWS, D_WIDTH), F32)
        h_ref[...] = jnp.zeros(h_ref.shape, F32)

    x = dx_ref[...]
    pad_ref[PAD_ROWS:PAD_ROWS + tb, :] = x
    cw = cw_ref[...]
    xc = pad_ref[PAD_ROWS - 3:PAD_ROWS - 3 + tb, :] * cw[0:1]
    xc = xc + pad_ref[PAD_ROWS - 2:PAD_ROWS - 2 + tb, :] * cw[1:2]
    xc = xc + pad_ref[PAD_ROWS - 1:PAD_ROWS - 1 + tb, :] * cw[2:3]
    xc = xc + x * cw[3:4]
    xc = xc + cb_ref[...]
    a, u = _lru_coeffs(xc, wr_ref[...], br_ref[...], wi_ref[...], bi_ref[...], lam_ref[...])
    row = lax.broadcasted_iota(jnp.int32, a.shape, 0)
    d = 1
    while d < tb:
        keep = row >= d
        a_prev = jnp.where(keep, pltpu.roll(a, d, 0), 1.0)
        u_prev = jnp.where(keep, pltpu.roll(u, d, 0), 0.0)
        u = a * u_prev + u
        a = a * a_prev
        d *= 2
    h = a * h_ref[0:1, :] + u
    y_ref[...] = (jax.nn.gelu(dg_ref[...]) * h).astype(y_ref.dtype)
    h_ref[0:1, :] = h[tb - 1:tb, :]
    hlast_ref[...] = h[tb - 1:tb, :]
    dtail_ref[...] = pad_ref[PAD_ROWS + tb - (D_CONV - 1):PAD_ROWS + tb, :]
    pad_ref[0:PAD_ROWS, :] = pad_ref[tb:tb + PAD_ROWS, :]


def _rglru_prompt(z3, cw, cb, wr, br, wi, bi, lam):
    bn, s_len, _ = z3.shape
    tb = min(256, s_len)
    nt = s_len // tb
    spec = lambda name: pl.BlockSpec((None, tb, 512), lambda b, t: (b, t, _blk(name, 512)))
    vec = pl.BlockSpec((1, D_WIDTH), lambda b, t: (0, 0))
    mat = pl.BlockSpec((D_WIDTH, D_WIDTH), lambda b, t: (0, 0))
    return pl.pallas_call(
        _rglru_kernel,
        grid=(bn, nt),
        in_specs=[spec("dx"), spec("dg"), pl.BlockSpec((D_CONV, D_WIDTH), lambda b, t: (0, 0)),
                  vec, mat, vec, mat, vec, vec],
        out_specs=[pl.BlockSpec((None, tb, D_WIDTH), lambda b, t: (b, t, 0)),
                   pl.BlockSpec((None, D_CONV - 1, D_WIDTH), lambda b, t: (b, 0, 0)),
                   pl.BlockSpec((None, 1, D_WIDTH), lambda b, t: (b, 0, 0))],
        out_shape=[jax.ShapeDtypeStruct((bn, s_len, D_WIDTH), BF16),
                   jax.ShapeDtypeStruct((bn, D_CONV - 1, D_WIDTH), F32),
                   jax.ShapeDtypeStruct((bn, 1, D_WIDTH), F32)],
        scratch_shapes=[pltpu.VMEM((PAD_ROWS + tb, D_WIDTH), F32),
                        pltpu.VMEM((8, D_WIDTH), F32)],
        compiler_params=_cparams(("parallel", "arbitrary")),
        name="rglru_prompt",
    )(z3, z3, cw, cb, wr, br, wi, bi, lam)


def _sample_bd_kernel(gb_ref, gc_ref, xb_ref, dg_ref, dx_ref, sb_ref, sd_ref, h0_ref,
                      bw_ref, cw_ref, cb_ref, wr_ref, br_ref, wi_ref, bi_ref, lam_ref,
                      yb_ref, yd_ref, bnew_ref, dnew_ref, hnew_ref):
    w = B_WIDTH
    p = gc_ref[...] * xb_ref[...]
    bw = bw_ref[...]
    b0 = sb_ref[:, 0:w]
    b1 = sb_ref[:, w:2 * w]
    y = b0 * bw[0:1] + b1 * bw[1:2] + p * bw[2:3]
    yb_ref[...] = (gb_ref[...] * y).astype(yb_ref.dtype)
    bnew_ref[:, 0:w] = b1
    bnew_ref[:, w:2 * w] = p

    w = D_WIDTH
    x = dx_ref[...]
    cw = cw_ref[...]
    d0 = sd_ref[:, 0:w]
    d1 = sd_ref[:, w:2 * w]
    d2 = sd_ref[:, 2 * w:3 * w]
    xc = d0 * cw[0:1]
    xc = xc + d1 * cw[1:2]
    xc = xc + d2 * cw[2:3]
    xc = xc + x * cw[3:4]
    xc = xc + cb_ref[...]
    a, u = _lru_coeffs(xc, wr_ref[...], br_ref[...], wi_ref[...], bi_ref[...], lam_ref[...])
    h = a * h0_ref[...] + u
    hnew_ref[...] = h
    yd_ref[...] = (jax.nn.gelu(dg_ref[...]) * h).astype(yd_ref.dtype)
    dnew_ref[:, 0:w] = d1
    dnew_ref[:, w:2 * w] = d2
    dnew_ref[:, 2 * w:3 * w] = x


def _sample_bd(z, sb, sd, h0, bw, cw, cb, wr, br, wi, bi, lam):
    m = z.shape[0]
    zspec = lambda name: pl.BlockSpec((m, 512), lambda i: (0, _blk(name, 512)))
    full = lambda a: pl.BlockSpec(a.shape, lambda i: (0,) * a.ndim)
    args = (sb, sd, h0, bw, cw, cb, wr, br, wi, bi, lam)
    out_shapes = [jax.ShapeDtypeStruct((m, B_WIDTH), BF16),
                  jax.ShapeDtypeStruct((m, D_WIDTH), BF16),
                  jax.ShapeDtypeStruct((m, (B_CONV - 1) * B_WIDTH), F32),
                  jax.ShapeDtypeStruct((m, (D_CONV - 1) * D_WIDTH), F32),
                  jax.ShapeDtypeStruct((m, D_WIDTH), F32)]
    return pl.pallas_call(
        _sample_bd_kernel,
        grid=(1,),
        in_specs=[zspec("gb"), zspec("gc"), zspec("xb"), zspec("dg"), zspec("dx")]
                 + [full(a) for a in args],
        out_specs=[pl.BlockSpec(s.shape, lambda i: (0, 0)) for s in out_shapes],
        out_shape=out_shapes,
        compiler_params=_cparams(("arbitrary",)),
        name="sample_bd",
    )(z, z, z, z, z, *args)


def _merge_kernel(ya_ref, yb_ref, yc_ref, yd_ref, g0_ref, g1_ref, g2_ref, g3_ref, x_ref,
                  wa_ref, wb_ref, wc_ref, wd_ref, wo_ref, gp_ref, o_ref):
    acc = jax.nn.sigmoid(g0_ref[...]) * _dot(ya_ref[...], wa_ref[...])
    acc = acc + jax.nn.sigmoid(g1_ref[...]) * _dot(yb_ref[...], wb_ref[...])
    acc = acc + jax.nn.sigmoid(g2_ref[...]) * _dot(yc_ref[...], wc_ref[...])
    acc = acc + jax.nn.sigmoid(g3_ref[...]) * _dot(yd_ref[...], wd_ref[...])
    m = _dot(acc.astype(BF16), wo_ref[...])
    o_ref[...] = x_ref[...] + _rms(m, gp_ref[...])


def _merge(ya, yb, yc, yd, z, x, wa, wb, wc, wd, wo, gp, tm):
    m = x.shape[0]
    tm = min(tm, m)
    row = lambda w: pl.BlockSpec((tm, w), lambda i: (i, 0))
    gate = lambda k: pl.BlockSpec((tm, D_MODEL), lambda i: (i, _blk("gt", D_MODEL) + k))
    const = lambda a: pl.BlockSpec(a.shape, lambda i: (0, 0))
    return pl.pallas_call(
        _merge_kernel,
        grid=(m // tm,),
        in_specs=[row(512), row(512), row(512), row(512), gate(0), gate(1), gate(2), gate(3),
                  row(D_MODEL), const(wa), const(wb), const(wc), const(wd), const(wo), const(gp)],
        out_specs=row(D_MODEL),
        out_shape=jax.ShapeDtypeStruct((m, D_MODEL), F32),
        compiler_params=_cparams(("parallel",)),
        name="merge",
    )(ya, yb, yc, yd, z, z, z, z, x, wa, wb, wc, wd, wo, gp)


def _ffn_kernel(x_ref, gpre_ref, wg_ref, wu_ref, wd_ref, gpost_ref, o_ref, h_ref, acc_ref, *, nf):
    j = pl.program_id(1)

    @pl.when(j == 0)
    def _():
        h_ref[...] = _rms(x_ref[...], gpre_ref[...]).astype(BF16)
        acc_ref[...] = jnp.zeros(acc_ref.shape, F32)

    h = h_ref[...]
    act = jax.nn.silu(_dot(h, wg_ref[...])) * _dot(h, wu_ref[...])
    acc_ref[...] += _dot(act.astype(BF16), wd_ref[...])

    @pl.when(j == nf - 1)
    def _():
        o_ref[...] = x_ref[...] + _rms(acc_ref[...], gpost_ref[...])


def _ffn(x, gpre, wg, wu, wd, gpost, tm, tf=256):
    m = x.shape[0]
    tm = min(tm, m)
    nf = D_FF // tf
    return pl.pallas_call(
        functools.partial(_ffn_kernel, nf=nf),
        grid=(m // tm, nf),
        in_specs=[pl.BlockSpec((tm, D_MODEL), lambda i, j: (i, 0)),
                  pl.BlockSpec((1, D_MODEL), lambda i, j: (0, 0)),
                  pl.BlockSpec((D_MODEL, tf), lambda i, j: (0, j)),
                  pl.BlockSpec((D_MODEL, tf), lambda i, j: (0, j)),
                  pl.BlockSpec((tf, D_MODEL), lambda i, j: (j, 0)),
                  pl.BlockSpec((1, D_MODEL), lambda i, j: (0, 0))],
        out_specs=pl.BlockSpec((tm, D_MODEL), lambda i, j: (i, 0)),
        out_shape=jax.ShapeDtypeStruct((m, D_MODEL), F32),
        scratch_shapes=[pltpu.VMEM((tm, D_MODEL), BF16), pltpu.VMEM((tm, D_MODEL), F32)],
        compiler_params=_cparams(("parallel", "arbitrary")),
        name="ffn",
    )(x, gpre, wg, wu, wd, gpost)


IDX_ROWS = 8
PAGES_PER_SCORE_STEP = 32
PAGES_PER_ATTN_STEP = 32


def _sample_scores_kernel(pt_ref, iq_ref, iw_ref, ikn_ref, *rest, g_pages, n_pages):
    ki_refs = rest[:g_pages]
    o_ref = rest[g_pages]
    j = pl.program_id(1)
    qi = iq_ref[...].astype(BF16)
    wi = iw_ref[...]
    rel = jnp.concatenate([_dot(qi, ki_refs[g][...].astype(BF16)) for g in range(g_pages)], axis=-1)
    sc = jnp.sum(wi * jnp.maximum(rel, 0.0), axis=0, keepdims=True)
    start = pl.multiple_of(j * (g_pages * PAGE_SIZE), g_pages * PAGE_SIZE)
    o_ref[:, pl.ds(start, g_pages * PAGE_SIZE)] = sc

    @pl.when(j == n_pages // g_pages - 1)
    def _():
        kn = ikn_ref[...].astype(BF16).astype(F32)
        rel = jnp.maximum(jnp.sum(qi.astype(F32) * kn, axis=1, keepdims=True), 0.0)
        sc = jnp.sum(wi * rel, axis=0, keepdims=True)
        lane = lax.broadcasted_iota(jnp.int32, (1, PAGE_SIZE), 1)
        o_ref[:, n_pages * PAGE_SIZE:(n_pages + 1) * PAGE_SIZE] = jnp.where(lane == 0, sc, -jnp.inf)


def _sample_scores(pt_flat, iq3, iw3, ikn3, cache_ki, layer, n_pages, g_pages):
    bd = iq3.shape[0]
    ns = (n_pages + 1) * PAGE_SIZE

    def page_map(g):
        return lambda b, j, pt: (layer, pt[b * n_pages + j * g_pages + g], 0, 0)

    grid_spec = pltpu.PrefetchScalarGridSpec(
        num_scalar_prefetch=1,
        grid=(bd, n_pages // g_pages),
        in_specs=[pl.BlockSpec((None, IDX_ROWS, IDX_DIM), lambda b, j, pt: (b, 0, 0)),
                  pl.BlockSpec((None, IDX_ROWS, 1), lambda b, j, pt: (b, 0, 0)),
                  pl.BlockSpec((None, 1, IDX_DIM), lambda b, j, pt: (b, 0, 0))]
                 + [pl.BlockSpec((None, None, IDX_DIM, PAGE_SIZE), page_map(g)) for g in range(g_pages)],
        out_specs=pl.BlockSpec((None, 1, ns), lambda b, j, pt: (b, 0, 0)),
    )
    return pl.pallas_call(
        functools.partial(_sample_scores_kernel, g_pages=g_pages, n_pages=n_pages),
        grid_spec=grid_spec,
        out_shape=jax.ShapeDtypeStruct((bd, 1, ns), F32),
        compiler_params=_cparams(("parallel", "arbitrary")),
        name="sample_scores",
    )(pt_flat, iq3, iw3, ikn3, *([cache_ki] * g_pages))


def _sample_select_kernel(s_ref, o_ref, *, topk, nbits, n_valid):
    xs0 = s_ref[...]
    col = lax.broadcasted_iota(jnp.int32, xs0.shape, 1)
    allowed = col < n_valid
    xs = jnp.where(allowed, xs0 + 0.0, -jnp.inf)
    sel = _topk_select(xs, allowed, col, topk, nbits, N_BISECT_SAMPLE)
    o_ref[...] = jnp.where(sel, 1.0, 0.0)


def _sample_select(scores, topk, n_valid):
    bd, ns = scores.shape
    nbits = max(1, (ns - 1).bit_length())
    return pl.pallas_call(
        functools.partial(_sample_select_kernel, topk=topk, nbits=nbits, n_valid=n_valid),
        grid=(1,),
        in_specs=[pl.BlockSpec((bd, ns), lambda i: (0, 0))],
        out_specs=pl.BlockSpec((bd, ns), lambda i: (0, 0)),
        out_shape=jax.ShapeDtypeStruct((bd, ns), F32),
        compiler_params=_cparams(("arbitrary",)),
        name="sample_select",
    )(scores)


def _online_update(s, weighted_values, m_ref, l_ref, acc_ref):
    m_old = m_ref[...]
    m_new = jnp.maximum(m_old, jnp.max(s, axis=1, keepdims=True))
    alpha = jnp.exp(m_old - m_new)
    p = jnp.exp(s - m_new)
    l_ref[...] = alpha * l_ref[...] + _rowsum(p)
    acc_ref[...] = alpha * acc_ref[...] + weighted_values(p.astype(BF16))
    m_ref[...] = m_new


def _online_update_one(s, v_row, m_ref, l_ref, acc_ref):
    m_old = m_ref[...]
    m_new = jnp.maximum(m_old, s)
    alpha = jnp.exp(m_old - m_new)
    p = jnp.exp(s - m_new)
    l_ref[...] = alpha * l_ref[...] + p
    acc_ref[...] = alpha * acc_ref[...] + p.astype(BF16).astype(F32) * v_row
    m_ref[...] = m_new


def _sample_attn_kernel(pt_ref, mask_ref, qa_ref, ql_ref, qr_ref, kn_ref, vn_ref, cn_ref, rn_ref,
                        *rest, g_pages, n_pages):
    k_refs = rest[0:g_pages]
    v_refs = rest[g_pages:2 * g_pages]
    c_refs = rest[2 * g_pages:3 * g_pages]
    r_refs = rest[3 * g_pages:4 * g_pages]
    oa_ref, oc_ref, ma_ref, la_ref, acca_ref, mc_ref, lc_ref, accc_ref = rest[4 * g_pages:]
    j = pl.program_id(1)

    @pl.when(j == 0)
    def _():
        for m_ref, l_ref, acc_ref in ((ma_ref, la_ref, acca_ref), (mc_ref, lc_ref, accc_ref)):
            m_ref[...] = jnp.full(m_ref.shape, NEG_BIG, F32)
            l_ref[...] = jnp.zeros(l_ref.shape, F32)
            acc_ref[...] = jnp.zeros(acc_ref.shape, F32)

    qa = qa_ref[...].astype(BF16)
    ql = ql_ref[...]
    qr = qr_ref[...].astype(BF16)
    span = g_pages * PAGE_SIZE
    start = pl.multiple_of(j * span, span)
    page = lambda x, g: x[:, g * PAGE_SIZE:(g + 1) * PAGE_SIZE]

    member = mask_ref[:, pl.ds(start, span)] > 0.5
    s = jnp.concatenate([_dot(qa, k_refs[g][...].astype(BF16)) for g in range(g_pages)], axis=-1)
    s = jnp.where(member, s * A_SCALE, -jnp.inf)
    _online_update(s, lambda p: sum(_dot_nt(page(p, g), v_refs[g][...].astype(BF16))
                                    for g in range(g_pages)), ma_ref, la_ref, acca_ref)

    cks = [c_refs[g][...].astype(BF16) for g in range(g_pages)]
    s = jnp.concatenate([_dot_nt(ql, cks[g]) + _dot(qr, r_refs[g][...].astype(BF16))
                         for g in range(g_pages)], axis=-1) * C_SCALE
    _online_update(s, lambda p: sum(_dot(page(p, g), cks[g]) for g in range(g_pages)),
                   mc_ref, lc_ref, accc_ref)

    @pl.when(j == n_pages // g_pages - 1)
    def _():
        rnd = lambda x: x.astype(BF16).astype(F32)
        member = mask_ref[:, n_pages * PAGE_SIZE:n_pages * PAGE_SIZE + 1] > 0.5
        s = jnp.sum(rnd(qa_ref[...]) * rnd(kn_ref[...]), axis=1, keepdims=True) * A_SCALE
        s = jnp.where(member, s, -jnp.inf)
        _online_update_one(s, rnd(vn_ref[...]), ma_ref, la_ref, acca_ref)
        cn = rnd(cn_ref[...])
        s = (jnp.sum(ql.astype(F32) * cn, axis=1, keepdims=True)
             + jnp.sum(rnd(qr_ref[...]) * rnd(rn_ref[...]), axis=1, keepdims=True)) * C_SCALE
        _online_update_one(s, cn, mc_ref, lc_ref, accc_ref)
        o = acca_ref[...] / la_ref[...]
        first_group = lax.broadcasted_iota(jnp.int32, oa_ref.shape, 0) < A_HEADS // A_KV_HEADS
        oa_ref[...] = jnp.where(first_group, o[:, :A_HEAD_DIM], o[:, A_HEAD_DIM:])
        oc_ref[...] = accc_ref[...] / lc_ref[...]


def _sample_attn(pt_flat, mask3, qa, ql, qr, kn, vn, cn, rn, cache_k, cache_v, cache_c, cache_r,
                 layer, n_pages, g_pages):
    bd = qa.shape[0]
    ns = mask3.shape[-1]

    def page_map(g):
        return lambda b, j, pt: (layer, pt[b * n_pages + j * g_pages + g], 0, 0)

    per_b = lambda r, w: pl.BlockSpec((None, r, w), lambda b, j, pt: (b, 0, 0))
    pages = lambda r, w: [pl.BlockSpec((None, None, r, w), page_map(g)) for g in range(g_pages)]
    grid_spec = pltpu.PrefetchScalarGridSpec(
        num_scalar_prefetch=1,
        grid=(bd, n_pages // g_pages),
        in_specs=[per_b(1, ns), per_b(A_HEADS, LANES), per_b(C_HEADS, C_KV_RANK), per_b(C_HEADS, C_ROPE),
                  per_b(1, LANES), per_b(1, LANES), per_b(1, C_KV_RANK), per_b(1, C_ROPE)]
                 + pages(A_KV_HEADS * A_HEAD_DIM, PAGE_SIZE) + pages(A_KV_HEADS * A_HEAD_DIM, PAGE_SIZE)
                 + pages(PAGE_SIZE, C_KV_RANK) + pages(C_ROPE, PAGE_SIZE),
        out_specs=[per_b(A_HEADS, A_HEAD_DIM), per_b(C_HEADS, C_KV_RANK)],
        scratch_shapes=[pltpu.VMEM((A_HEADS, 1), F32), pltpu.VMEM((A_HEADS, 1), F32),
                        pltpu.VMEM((A_HEADS, LANES), F32),
                        pltpu.VMEM((C_HEADS, 1), F32), pltpu.VMEM((C_HEADS, 1), F32),
                        pltpu.VMEM((C_HEADS, C_KV_RANK), F32)],
    )
    return pl.pallas_call(
        functools.partial(_sample_attn_kernel, g_pages=g_pages, n_pages=n_pages),
        grid_spec=grid_spec,
        out_shape=[jax.ShapeDtypeStruct((bd, A_HEADS, A_HEAD_DIM), F32),
                   jax.ShapeDtypeStruct((bd, C_HEADS, C_KV_RANK), F32)],
        compiler_params=_cparams(("parallel", "arbitrary")),
        name="sample_attn",
    )(pt_flat, mask3, qa, ql, qr, kn, vn, cn, rn,
      *([cache_k] * g_pages), *([cache_v] * g_pages), *([cache_c] * g_pages), *([cache_r] * g_pages))


def _uv_kernel(o_ref, wuv_ref, y_ref):
    w = 2 * C_KV_RANK
    outs = [_dot(o_ref[:, w * p:w * (p + 1)].astype(BF16), wuv_ref[p]) for p in range(C_HEADS // 2)]
    y_ref[...] = jnp.concatenate(outs, axis=-1).astype(y_ref.dtype)


def _uv_proj(o_lat, wuvp):
    m = o_lat.shape[0]
    return pl.pallas_call(
        _uv_kernel,
        grid=(1,),
        in_specs=[pl.BlockSpec(o_lat.shape, lambda i: (0, 0)),
                  pl.BlockSpec(wuvp.shape, lambda i: (0, 0, 0))],
        out_specs=pl.BlockSpec((m, C_WIDTH), lambda i: (0, 0)),
        out_shape=jax.ShapeDtypeStruct((m, C_WIDTH), BF16),
        compiler_params=_cparams(("arbitrary",)),
        name="uv_proj",
    )(o_lat, wuvp)


def _rope_tables(pos):
    half = C_ROPE // 2
    freqs = ROPE_THETA ** (-jnp.arange(half, dtype=F32) / half)
    ang = pos.astype(F32)[:, None] * freqs[None, :]
    return jnp.tile(jnp.cos(ang), (1, C_HEADS)), jnp.tile(jnp.sin(ang), (1, C_HEADS))


def _layer_weights(l, w_in, w_out_a, conv_b, w_out_b, g_cq, w_uq, g_ckv, w_uk, w_uv, w_out_c,
                   conv_d_w, conv_d_b, w_rg, b_rg, w_ig, b_ig, lru_lambda, w_out_d, w_o,
                   w_ffn_gate, w_ffn_up, w_ffn_down):
    lw = {}
    lw["w_in"] = _relayout_w_in(w_in[l])
    uq = w_uq[l].reshape(C_Q_RANK, C_HEADS, C_NOPE + C_ROPE)
    half = C_ROPE // 2
    lw["w_uq"] = jnp.concatenate(
        [uq[:, :, :C_NOPE].reshape(C_Q_RANK, -1),
         uq[:, :, C_NOPE:C_NOPE + half].reshape(C_Q_RANK, -1),
         uq[:, :, C_NOPE + half:].reshape(C_Q_RANK, -1)], axis=1).astype(BF16)
    ukt = jnp.transpose(w_uk[l], (1, 2, 0))
    zk = jnp.zeros((C_NOPE, C_KV_RANK), F32)
    lw["w_uk"] = jnp.stack([
        jnp.concatenate([jnp.concatenate([ukt[2 * p], zk], axis=1),
                         jnp.concatenate([zk, ukt[2 * p + 1]], axis=1)], axis=0)
        for p in range(C_HEADS // 2)]).astype(BF16)
    uv = jnp.transpose(w_uv[l], (1, 0, 2))
    zv = jnp.zeros((C_KV_RANK, C_VDIM), F32)
    lw["w_uv"] = jnp.stack([
        jnp.concatenate([jnp.concatenate([uv[2 * p], zv], axis=1),
                         jnp.concatenate([zv, uv[2 * p + 1]], axis=1)], axis=0)
        for p in range(C_HEADS // 2)]).astype(BF16)
    eye = jnp.eye(D_BLOCKS, dtype=F32)
    bd = lambda w: (eye[:, None, :, None] * w[:, :, None, :]).reshape(D_WIDTH, D_WIDTH).astype(BF16)
    lw["w_rg"] = bd(w_rg[l])
    lw["w_ig"] = bd(w_ig[l])
    row = lambda v: v[l][None, :]
    lw["b_rg"], lw["b_ig"], lw["lam"], lw["conv_d_b"] = row(b_rg), row(b_ig), row(lru_lambda), row(conv_d_b)
    lw["g_cq"], lw["g_ckv"] = row(g_cq), row(g_ckv)
    lw["conv_b"], lw["conv_d_w"] = conv_b[l], conv_d_w[l]
    for name, w in (("w_out_a", w_out_a), ("w_out_b", w_out_b), ("w_out_c", w_out_c),
                    ("w_out_d", w_out_d), ("w_o", w_o), ("w_ffn_gate", w_ffn_gate),
                    ("w_ffn_up", w_ffn_up), ("w_ffn_down", w_ffn_down)):
        lw[name] = w[l].astype(BF16)
    return lw


def _zcols(z, name, lo, hi):
    o = _DST[name][0]
    return z[..., o + lo:o + hi]


def _state_from_z(z):
    hd = A_HEAD_DIM
    ak = jnp.concatenate([_zcols(z, "ak", 0, hd), _zcols(z, "ak", 2 * hd, 3 * hd)], axis=-1)
    av = jnp.concatenate([_zcols(z, "av", 0, hd), _zcols(z, "av", 2 * hd, 3 * hd)], axis=-1)
    ik = _zcols(z, "ik", 0, IDX_DIM)
    return ak, av, ik


def _krope_state(kr):
    half = C_ROPE // 2
    return jnp.concatenate([kr[:, :half], kr[:, LANES:LANES + half]], axis=-1)


def _prompt_layer(x, lw, gains, bn, s_len, cos, sin):
    g_mix_pre, g_mix_post, g_ffn_pre, g_ffn_post = gains
    m = bn * s_len
    z = _inproj(x, g_mix_pre, lw["w_in"], tm=1024)
    z3 = z.reshape(bn, s_len, N_PROJ)
    tm_c = min(512, s_len)
    nb = s_len // tm_c
    q, kcat, ckvn, kr, _ = _cprep(z, cos, sin, lambda i: (i % nb, 0), lw["g_cq"], lw["g_ckv"],
                                  lw["w_uq"], lw["w_uk"], tm=tm_c)
    yc = _mla_prompt(q, kcat, lw["w_uv"], bn, s_len)
    ya = _sparse_prompt(z3, min(TOPK_MAX, s_len // 4)).reshape(m, A_WIDTH)
    yb, b_new = _bconv_prompt(z3, lw["conv_b"])
    yd, d_new, h_new = _rglru_prompt(z3, lw["conv_d_w"], lw["conv_d_b"], lw["w_rg"], lw["b_rg"],
                                     lw["w_ig"], lw["b_ig"], lw["lam"])
    x = _merge(ya, yb.reshape(m, B_WIDTH), yc, yd.reshape(m, D_WIDTH), z, x,
               lw["w_out_a"], lw["w_out_b"], lw["w_out_c"], lw["w_out_d"], lw["w_o"], g_mix_post, tm=512)
    x = _ffn(x, g_ffn_pre, lw["w_ffn_gate"], lw["w_ffn_up"], lw["w_ffn_down"], g_ffn_post, tm=1024)
    ak, av, ik = _state_from_z(z3)
    st = (ak.reshape(bn, s_len, A_KV_HEADS, A_HEAD_DIM), av.reshape(bn, s_len, A_KV_HEADS, A_HEAD_DIM),
          ik, ckvn.reshape(bn, s_len, C_KV_RANK), _krope_state(kr).reshape(bn, s_len, C_ROPE),
          b_new, d_new, h_new.reshape(bn, D_WIDTH))
    return x, st


def _sample_layer(x, lw, gains, layer, caches, states, pt_flat, n_pages, cos, sin, g_pages):
    g_mix_pre, g_mix_post, g_ffn_pre, g_ffn_post = gains
    cache_k, cache_v, cache_ki, cache_c, cache_r = caches
    sb, sd, h0 = states
    bd = x.shape[0]
    z = _inproj(x, g_mix_pre, lw["w_in"], tm=bd)
    q, _, ckvn, kr, qrope = _cprep(z, cos, sin, lambda i: (0, 0), lw["g_cq"], lw["g_ckv"],
                                   lw["w_uq"], lw["w_uk"], tm=bd)
    ak, av, ik = _state_from_z(z)
    krope = _krope_state(kr)

    iq3 = jnp.pad(_zcols(z, "iq", 0, 256).reshape(bd, IDX_HEADS, IDX_DIM),
                  ((0, 0), (0, IDX_ROWS - IDX_HEADS), (0, 0)))
    iw3 = jnp.pad(_zcols(z, "iw", 0, IDX_HEADS), ((0, 0), (0, IDX_ROWS - IDX_HEADS)))[:, :, None]
    scores = _sample_scores(pt_flat, iq3, iw3, ik[:, None, :], cache_ki, layer, n_pages, g_pages[0])
    total = n_pages * PAGE_SIZE + 1
    mask = _sample_select(scores.reshape(bd, -1), min(TOPK_MAX, total // 4), total)

    aq = _zcols(z, "aq", 0, A_WIDTH).reshape(bd, A_KV_HEADS, A_HEADS // A_KV_HEADS, A_HEAD_DIM)
    zq = jnp.zeros_like(aq[:, 0])
    qa = jnp.concatenate([jnp.concatenate([aq[:, 0], zq], axis=-1),
                          jnp.concatenate([zq, aq[:, 1]], axis=-1)], axis=1)
    ql = jnp.transpose(q[:, :, :C_KV_RANK], (1, 0, 2))
    half = C_ROPE // 2
    qr = jnp.concatenate([qrope[:, :LANES].reshape(bd, C_HEADS, half),
                          qrope[:, LANES:].reshape(bd, C_HEADS, half)], axis=-1)
    oa, oc = _sample_attn(pt_flat, mask[:, None, :], qa, ql, qr, ak[:, None, :], av[:, None, :],
                          ckvn[:, None, :], krope[:, None, :], cache_k, cache_v, cache_c, cache_r,
                          layer, n_pages, g_pages[1])
    ya = oa.reshape(bd, A_WIDTH).astype(BF16)
    yc = _uv_proj(oc.reshape(bd, C_HEADS * C_KV_RANK), lw["w_uv"])

    yb, yd, b_new, d_new, h_new = _sample_bd(
        z, sb.reshape(bd, -1), sd.reshape(bd, -1), h0, lw["conv_b"], lw["conv_d_w"], lw["conv_d_b"],
        lw["w_rg"], lw["b_rg"], lw["w_ig"], lw["b_ig"], lw["lam"])
    x = _merge(ya, yb, yc, yd, z, x, lw["w_out_a"], lw["w_out_b"], lw["w_out_c"], lw["w_out_d"],
               lw["w_o"], g_mix_post, tm=bd)
    x = _ffn(x, g_ffn_pre, lw["w_ffn_gate"], lw["w_ffn_up"], lw["w_ffn_down"], g_ffn_post, tm=bd)
    st = (ak.reshape(bd, 1, A_KV_HEADS, A_HEAD_DIM), av.reshape(bd, 1, A_KV_HEADS, A_HEAD_DIM),
          ik[:, None, :], ckvn[:, None, :], krope[:, None, :],
          b_new.reshape(bd, B_CONV - 1, B_WIDTH), d_new.reshape(bd, D_CONV - 1, D_WIDTH), h_new)
    return x, st


def kernel(x_prompt, x_sample, cache_a_k, cache_a_v, cache_a_kidx, cache_c_kv, cache_c_krope, state_b_conv, state_d_conv, state_d_h, page_table, g_mix_pre, g_mix_post, g_ffn_pre, g_ffn_post, w_in, w_out_a, conv_b, w_out_b, g_cq, w_uq, g_ckv, w_uk, w_uv, w_out_c, conv_d_w, conv_d_b, w_rg, b_rg, w_ig, b_ig, lru_lambda, w_out_d, w_o, w_ffn_gate, w_ffn_up, w_ffn_down):
    bn, s_len, _ = x_prompt.shape
    bd, t_len, _ = x_sample.shape
    assert t_len == 1 and s_len % Q_BLOCK == 0
    depth = w_in.shape[0]
    n_pages = page_table.shape[1]
    n_pool = cache_a_k.shape[1]
    past = n_pages * PAGE_SIZE
    largest_group = lambda cap: max(g for g in range(1, cap + 1) if n_pages % g == 0)
    g_pages = (largest_group(PAGES_PER_SCORE_STEP), largest_group(PAGES_PER_ATTN_STEP))

    cos_p, sin_p = _rope_tables(jnp.arange(s_len))
    cos_s, sin_s = _rope_tables(jnp.full((bd,), past))
    pt_flat = page_table.reshape(-1)
    kv_t = lambda c: jnp.transpose(c, (0, 1, 3, 4, 2)).reshape(
        depth, n_pool, A_KV_HEADS * A_HEAD_DIM, PAGE_SIZE)
    caches = (kv_t(cache_a_k), kv_t(cache_a_v), jnp.transpose(cache_a_kidx, (0, 1, 3, 2)),
              cache_c_kv, jnp.transpose(cache_c_krope, (0, 1, 3, 2)))

    xp = x_prompt.reshape(bn * s_len, D_MODEL)
    xs = x_sample.reshape(bd, D_MODEL)
    p_states, s_states = [], []
    for l in range(depth):
        lw = _layer_weights(l, w_in, w_out_a, conv_b, w_out_b, g_cq, w_uq, g_ckv, w_uk, w_uv, w_out_c,
                            conv_d_w, conv_d_b, w_rg, b_rg, w_ig, b_ig, lru_lambda, w_out_d, w_o,
                            w_ffn_gate, w_ffn_up, w_ffn_down)
        gains = tuple(g[l][None, :] for g in (g_mix_pre, g_mix_post, g_ffn_pre, g_ffn_post))
        xp, st_p = _prompt_layer(xp, lw, gains, bn, s_len, cos_p, sin_p)
        xs, st_s = _sample_layer(xs, lw, gains, l, caches,
                                 (state_b_conv[l], state_d_conv[l], state_d_h[l]),
                                 pt_flat, n_pages, cos_s, sin_s, g_pages)
        p_states.append(st_p)
        s_states.append(st_s)
    p_out = [jnp.stack([st[j] for st in p_states]) for j in range(8)]
    s_out = [jnp.stack([st[j] for st in s_states]) for j in range(8)]
    return (xp.reshape(bn, s_len, D_MODEL), xs.reshape(bd, 1, D_MODEL), *p_out, *s_out)
```

```python
import functools

import numpy as np
import jax
import jax.numpy as jnp
from jax import lax
from jax.experimental import pallas as pl
from jax.experimental.pallas import tpu as pltpu

D_MODEL = 1024
PAGE_SIZE = 128
A_HEADS = 8
A_KV_HEADS = 2
A_HEAD_DIM = 64
A_WIDTH = A_HEADS * A_HEAD_DIM
A_SCALE = A_HEAD_DIM ** -0.5
IDX_HEADS = 4
IDX_DIM = 64
TOPK_MAX = 256
B_WIDTH = 512
B_CONV = 3
C_HEADS = 8
C_NOPE = 64
C_ROPE = 32
C_VDIM = 64
C_Q_RANK = 256
C_KV_RANK = 256
C_WIDTH = C_HEADS * C_VDIM
C_SCALE = (C_NOPE + C_ROPE) ** -0.5
ROPE_THETA = 10000.0
D_WIDTH = 512
D_BLOCKS = 8
D_BLOCK = D_WIDTH // D_BLOCKS
D_CONV = 4
LRU_C = 8.0
N_BRANCH = 4
D_FF = ((8 * D_MODEL + 3 * 256 - 1) // (3 * 256)) * 256
Q_BLOCK = 128
EPS = 1e-6

LANES = 128
VMEM_LIMIT = 56 * 1024 * 1024
NEG_BIG = -1e30

BF16 = jnp.bfloat16
F32 = jnp.float32

_SRC = {}
_o = 0
for _n, _w in (("aq", A_WIDTH), ("ak", 128), ("av", 128), ("iq", 256), ("ik", 64), ("iw", 4),
               ("gb", 512), ("gc", 512), ("xb", 512), ("cq", 256), ("ckv", 256), ("ckr", 32),
               ("dg", 512), ("dx", 512), ("gt", 4096)):
    _SRC[_n] = (_o, _w)
    _o += _w
N_IN = _o

_DST = {}
_o = 0
for _n, _w in (("aq", 512), ("gb", 512), ("gc", 512), ("xb", 512), ("dg", 512), ("dx", 512),
               ("gt", 4096), ("iq", 256), ("cq", 256), ("ckv", 256), ("ckr", 256),
               ("ak", 256), ("av", 256), ("ik", 128), ("iw", 128)):
    _DST[_n] = (_o, _w)
    _o += _w
N_PROJ = _o
PROJ_TN = 1280


def _blk(name, width):
    off = _DST[name][0]
    assert off % width == 0
    return off // width


def _cparams(sem, vmem=VMEM_LIMIT):
    return pltpu.CompilerParams(dimension_semantics=sem, vmem_limit_bytes=vmem)


def _relayout_w_in(wt):
    def src(name, lo=0, hi=None):
        o, wd = _SRC[name]
        hi = wd if hi is None else hi
        return wt[o + lo:o + hi]

    zeros = lambda n: jnp.zeros((n, wt.shape[1]), wt.dtype)
    parts = [src("aq"), src("gb"), src("gc"), src("xb"), src("dg"), src("dx"), src("gt"),
             src("iq"), src("cq"), src("ckv"),
             jnp.tile(src("ckr", 0, 16), (8, 1)), jnp.tile(src("ckr", 16, 32), (8, 1)),
             src("ak", 0, 64), src("ak", 0, 64), src("ak", 64, 128), src("ak", 64, 128),
             src("av", 0, 64), src("av", 0, 64), src("av", 64, 128), src("av", 64, 128),
             src("ik"), src("ik"),
             src("iw"), zeros(124)]
    out = jnp.concatenate(parts, axis=0)
    assert out.shape[0] == N_PROJ
    return out.astype(BF16)


def _rms(x, g):
    inv = lax.rsqrt(jnp.mean(x * x, axis=-1, keepdims=True) + EPS)
    return (x * inv) * g


def _dot(a, b):
    return jnp.dot(a, b, preferred_element_type=F32)


def _dot_nt(a, b):
    return lax.dot_general(a, b, (((1,), (1,)), ((), ())), preferred_element_type=F32)


def _inproj_kernel(x_ref, g_ref, w_ref, o_ref, h_ref):
    @pl.when(pl.program_id(1) == 0)
    def _():
        h_ref[...] = _rms(x_ref[...], g_ref[...]).astype(BF16)

    o_ref[...] = _dot_nt(h_ref[...], w_ref[...])


def _inproj(x, g, w, tm):
    m = x.shape[0]
    tm = min(tm, m)
    return pl.pallas_call(
        _inproj_kernel,
        grid=(m // tm, N_PROJ // PROJ_TN),
        in_specs=[pl.BlockSpec((tm, D_MODEL), lambda i, j: (i, 0)),
                  pl.BlockSpec((1, D_MODEL), lambda i, j: (0, 0)),
                  pl.BlockSpec((PROJ_TN, D_MODEL), lambda i, j: (j, 0))],
        out_specs=pl.BlockSpec((tm, PROJ_TN), lambda i, j: (i, j)),
        out_shape=jax.ShapeDtypeStruct((m, N_PROJ), F32),
        scratch_shapes=[pltpu.VMEM((tm, D_MODEL), BF16)],
        compiler_params=_cparams(("parallel", "arbitrary")),
        name="in_proj",
    )(x, g, w)


def _cprep_kernel(cq_ref, ckv_ref, ckr_ref, cos_ref, sin_ref, gq_ref, gkv_ref, wuq_ref, wuk_ref,
                  q_ref, kcat_ref, ckvn_ref, kr_ref, qr_ref):
    cos = cos_ref[...]
    sin = sin_ref[...]
    cqn = _rms(cq_ref[...], gq_ref[...]).astype(BF16)
    q = _dot(cqn, wuq_ref[...])
    nope = C_HEADS * C_NOPE
    r1 = q[:, nope:nope + LANES]
    r2 = q[:, nope + LANES:nope + 2 * LANES]
    o1 = r1 * cos - r2 * sin
    o2 = r1 * sin + r2 * cos
    qr_ref[...] = jnp.concatenate([o1, o2], axis=-1)
    head_of_lane = lax.broadcasted_iota(jnp.int32, o1.shape, 1) // (C_ROPE // 2)
    for p in range(C_HEADS // 2):
        ql = _dot(q[:, LANES * p:LANES * (p + 1)].astype(BF16), wuk_ref[p])
        for e in range(2):
            h = 2 * p + e
            sel = head_of_lane == h
            q_ref[h] = jnp.concatenate(
                [ql[:, C_KV_RANK * e:C_KV_RANK * (e + 1)],
                 jnp.where(sel, o1, 0.0), jnp.where(sel, o2, 0.0)], axis=-1).astype(BF16)
    ckvn = _rms(ckv_ref[...], gkv_ref[...])
    ckvn_ref[...] = ckvn
    c = ckr_ref[...]
    c1 = c[:, :LANES]
    c2 = c[:, LANES:]
    k1 = c1 * cos - c2 * sin
    k2 = c1 * sin + c2 * cos
    kr_ref[...] = jnp.concatenate([k1, k2], axis=-1)
    kcat_ref[...] = jnp.concatenate([ckvn, k1, k2], axis=-1).astype(BF16)


def _cprep(z, cos, sin, tab_map, g_cq, g_ckv, wuq, wukp, tm):
    m = z.shape[0]
    tm = min(tm, m)
    qw = C_KV_RANK + 2 * LANES
    const2 = lambda i: (0, 0)
    return pl.pallas_call(
        _cprep_kernel,
        grid=(m // tm,),
        in_specs=[pl.BlockSpec((tm, 256), lambda i: (i, _blk("cq", 256))),
                  pl.BlockSpec((tm, 256), lambda i: (i, _blk("ckv", 256))),
                  pl.BlockSpec((tm, 256), lambda i: (i, _blk("ckr", 256))),
                  pl.BlockSpec((tm, LANES), tab_map),
                  pl.BlockSpec((tm, LANES), tab_map),
                  pl.BlockSpec((1, C_Q_RANK), const2),
                  pl.BlockSpec((1, C_KV_RANK), const2),
                  pl.BlockSpec(wuq.shape, const2),
                  pl.BlockSpec(wukp.shape, lambda i: (0, 0, 0))],
        out_specs=[pl.BlockSpec((C_HEADS, tm, qw), lambda i: (0, i, 0)),
                   pl.BlockSpec((tm, qw), lambda i: (i, 0)),
                   pl.BlockSpec((tm, C_KV_RANK), lambda i: (i, 0)),
                   pl.BlockSpec((tm, 2 * LANES), lambda i: (i, 0)),
                   pl.BlockSpec((tm, 2 * LANES), lambda i: (i, 0))],
        out_shape=[jax.ShapeDtypeStruct((C_HEADS, m, qw), BF16),
                   jax.ShapeDtypeStruct((m, qw), BF16),
                   jax.ShapeDtypeStruct((m, C_KV_RANK), F32),
                   jax.ShapeDtypeStruct((m, 2 * LANES), F32),
                   jax.ShapeDtypeStruct((m, 2 * LANES), F32)],
        compiler_params=_cparams(("parallel",)),
        name="c_prep",
    )(z, z, z, cos, sin, g_cq, g_ckv, wuq, wukp)


MLA_CHUNK_ROWS = 256
MLA_KEY_BLOCK = 512
MLA_QUERY_BLOCK = 256


def _mla_kernel(q_ref, k_ref, wuv_ref, o_ref, m_ref, l_ref, acc_ref, *, tk, nkb, qb):
    i = pl.program_id(1)
    j = pl.program_id(2)
    chunk_heads = max(1, MLA_CHUNK_ROWS // qb)
    chunk_rows = chunk_heads * qb

    @pl.when(j == 0)
    def _():
        m_ref[...] = jnp.full(m_ref.shape, NEG_BIG, F32)
        l_ref[...] = jnp.zeros(l_ref.shape, F32)
        acc_ref[...] = jnp.zeros(acc_ref.shape, F32)

    def step(masked):
        k = k_ref[...]
        v = k[:, :C_KV_RANK]
        n_chunk = C_HEADS // chunk_heads

        def scores(c):
            q = q_ref[c * chunk_heads:(c + 1) * chunk_heads].reshape(chunk_rows, q_ref.shape[-1])
            return _dot_nt(q, k) * C_SCALE

        lane_tiles = lambda x: [x[:, LANES * u:LANES * (u + 1)] for u in range(x.shape[1] // LANES)]
        s_next = scores(0)
        for c in range(n_chunk):
            r0, r1 = c * chunk_rows, (c + 1) * chunk_rows
            s = s_next
            if c + 1 < n_chunk:
                s_next = scores(c + 1)
            if masked:
                t = i * qb + lax.broadcasted_iota(jnp.int32, s.shape, 0) % qb
                col = j * tk + lax.broadcasted_iota(jnp.int32, s.shape, 1)
                s = jnp.where(col <= t, s, -jnp.inf)
            m_old = m_ref[r0:r1, :]
            m_new = jnp.maximum(m_old, jnp.max(s, axis=1, keepdims=True))
            alpha = jnp.exp(m_old - m_new)
            p_tiles = [jnp.exp(st - m_new) for st in lane_tiles(s)]
            l_ref[r0:r1, :] = alpha * l_ref[r0:r1, :] + sum(p_tiles)
            pv = _dot(jnp.concatenate(p_tiles, axis=-1).astype(BF16), v)
            alpha_wide = jnp.concatenate([alpha] * (C_KV_RANK // LANES), axis=-1)
            acc_ref[r0:r1, :] = alpha_wide * acc_ref[r0:r1, :] + pv
            m_ref[r0:r1, :] = m_new

    first_row = i * qb
    fully_visible = j * tk + tk - 1 <= first_row
    pl.when(fully_visible)(functools.partial(step, False))
    pl.when(jnp.logical_and(jnp.logical_not(fully_visible), j * tk <= first_row + qb - 1))(
        functools.partial(step, True))

    @pl.when(j == nkb - 1)
    def _():
        o = acc_ref[...] / jnp.sum(l_ref[...], axis=1, keepdims=True)
        outs = []
        for p in range(C_HEADS // 2):
            pair = jnp.concatenate([o[2 * p * qb:(2 * p + 1) * qb],
                                    o[(2 * p + 1) * qb:(2 * p + 2) * qb]], axis=-1)
            outs.append(_dot(pair.astype(BF16), wuv_ref[p]))
        o_ref[...] = jnp.concatenate(outs, axis=-1).astype(o_ref.dtype)


def _mla_prompt(q, kcat, wuvp, bn, s_len):
    qb = min(MLA_QUERY_BLOCK, s_len)
    nq = s_len // qb
    tk = min(MLA_KEY_BLOCK, s_len)
    nkb = s_len // tk
    qw = q.shape[-1]

    def k_map(b, i, j):
        last = (i * qb + qb - 1) // tk
        return (b * nkb + jnp.minimum(j, last), 0)

    return pl.pallas_call(
        functools.partial(_mla_kernel, tk=tk, nkb=nkb, qb=qb),
        grid=(bn, nq, nkb),
        in_specs=[pl.BlockSpec((C_HEADS, qb, qw), lambda b, i, j: (0, b * nq + i, 0)),
                  pl.BlockSpec((tk, qw), k_map),
                  pl.BlockSpec(wuvp.shape, lambda b, i, j: (0, 0, 0))],
        out_specs=pl.BlockSpec((qb, C_WIDTH), lambda b, i, j: (b * nq + i, 0)),
        out_shape=jax.ShapeDtypeStruct((bn * s_len, C_WIDTH), BF16),
        scratch_shapes=[pltpu.VMEM((C_HEADS * qb, LANES), F32),
                        pltpu.VMEM((C_HEADS * qb, LANES), F32),
                        pltpu.VMEM((C_HEADS * qb, C_KV_RANK), F32)],
        compiler_params=_cparams(("parallel", "parallel", "arbitrary")),
        name="mla_prompt",
    )(q, kcat, wuvp)


def _rowsum(x):
    return jnp.sum(x, axis=1, keepdims=True)


def _topk_select(xs, allowed, col, k, nbits, n_bisect):
    kf = float(k)
    inf = jnp.float32(jnp.inf)
    row_min = jnp.min(jnp.where(allowed, xs, inf), axis=1, keepdims=True)
    row_max = jnp.max(xs, axis=1, keepdims=True)
    n_allowed = _rowsum(jnp.where(allowed, 1.0, 0.0))

    def bisect(_, c):
        low, high, has_low = c
        mid = low + (high - low) * 0.5
        up = _rowsum(jnp.where(xs > mid, 1.0, 0.0)) >= kf
        return jnp.where(up, mid, low), jnp.where(up, high, mid), jnp.where(up, 1.0, has_low)

    low, high, has_low = lax.fori_loop(0, n_bisect, bisect, (row_min, row_max, jnp.zeros_like(row_min)))
    lo0 = jnp.min(jnp.where(xs > low, xs, inf), axis=1, keepdims=True)
    lo0 = jnp.where((has_low > 0.5) & (n_allowed > kf), lo0, row_min)
    ub0 = jnp.max(jnp.where(xs > high, -inf, xs), axis=1, keepdims=True)
    ub0 = jnp.where(n_allowed > kf, ub0, row_min)

    def cond(c):
        lo, ub = c
        return jnp.max(jnp.where(lo < ub, 1.0, 0.0)) > 0.0

    def body(c):
        lo, ub = c
        mid = lo + (ub - lo) * 0.5
        mid = jnp.where((mid >= lo) & (mid < ub), mid, lo)
        gt = xs > mid
        cnt = _rowsum(jnp.where(gt, 1.0, 0.0))
        vmin = jnp.min(jnp.where(gt, xs, inf), axis=1, keepdims=True)
        vmax = jnp.max(jnp.where(gt, -inf, xs), axis=1, keepdims=True)
        active = lo < ub
        up = cnt >= kf
        lo = jnp.where(active & up, vmin, lo)
        ub = jnp.where(active & jnp.logical_not(up), vmax, ub)
        return lo, ub

    v, _ = lax.while_loop(cond, body, (lo0, ub0))
    gtv = xs > v
    need = kf - _rowsum(jnp.where(gtv, 1.0, 0.0))
    tie = allowed & (xs == v)
    n_tie = _rowsum(jnp.where(tie, 1.0, 0.0))

    def search(_):
        def step(it, c):
            cand = c + jnp.left_shift(jnp.int32(1), nbits - 1 - it)
            f = _rowsum(jnp.where(tie & (col < cand), 1.0, 0.0))
            return jnp.where(f < need, cand, c)
        return lax.fori_loop(0, nbits, step, jnp.zeros(v.shape, jnp.int32))

    def no_search(_):
        return jnp.full(v.shape, (1 << nbits) - 1, jnp.int32)

    any_over = jnp.max(jnp.where(n_tie > need, 1.0, 0.0)) > 0.0
    cut = lax.cond(any_over, search, no_search, 0)
    return gtv | (tie & (col <= cut))


KEY_CLASS = 256
N_BISECT_PROMPT = 14
N_BISECT_SAMPLE = 16


def _sparse_prompt_kernel(q_ref, iq_ref, iw_ref, kd_ref, vd_ref, ikd_ref, o_ref, kb_ref, vb_ref, ib_ref,
                          *, topk, s_len):
    i = pl.program_id(1)

    @pl.when(i == 0)
    def _():
        ib_ref[...] = ikd_ref[...].astype(BF16)
        for g in range(A_KV_HEADS):
            kb_ref[g] = kd_ref[:, LANES * g:LANES * (g + 1)].astype(BF16)
            vb_ref[g] = vd_ref[:, LANES * g:LANES * (g + 1)].astype(BF16)

    first_half = lax.broadcasted_iota(jnp.int32, (Q_BLOCK, LANES), 1) < (LANES // 2)

    def half(x, pair, e):
        blk = x[:, LANES * pair:LANES * (pair + 1)]
        return jnp.where(first_half if e == 0 else jnp.logical_not(first_half), blk, 0.0).astype(BF16)

    def body(width):
        q = q_ref[...] * A_SCALE
        iq = iq_ref[...]
        iw = iw_ref[...]
        ikd = ib_ref[0:width, :]
        score = None
        for h in range(IDX_HEADS):
            rel = jnp.maximum(_dot_nt(half(iq, h // 2, h % 2), ikd), 0.0)
            term = iw[:, h:h + 1] * rel
            score = term if score is None else score + term
        t = i * Q_BLOCK + lax.broadcasted_iota(jnp.int32, score.shape, 0)
        col = lax.broadcasted_iota(jnp.int32, score.shape, 1)
        allowed = col <= t
        xs = jnp.where(allowed, score + 0.0, -jnp.inf)
        nbits = max(1, (width - 1).bit_length())
        sel = _topk_select(xs, allowed, col, topk, nbits, N_BISECT_PROMPT)
        bias = jnp.where(sel, 0.0, -jnp.inf)

        rep = A_HEADS // A_KV_HEADS
        outs = []
        for g in range(A_KV_HEADS):
            kd = kb_ref[g, 0:width, :]
            vd = vb_ref[g, 0:width, :]
            heads = [g * rep + r for r in range(rep)]
            s_all = _dot_nt(jnp.concatenate([half(q, h // 2, h % 2) for h in heads], axis=0), kd)
            ps, ls = [], []
            for r in range(rep):
                s = s_all[r * Q_BLOCK:(r + 1) * Q_BLOCK] + bias
                m = jnp.max(s, axis=1, keepdims=True)
                p = jnp.exp(s - m)
                ls.append(_rowsum(p))
                ps.append(p.astype(BF16))
            o_all = _dot(jnp.concatenate(ps, axis=0), vd)
            for r in range(rep):
                outs.append(o_all[r * Q_BLOCK:(r + 1) * Q_BLOCK] / ls[r])
        pairs = [jnp.where(first_half, outs[2 * j], outs[2 * j + 1]) for j in range(A_HEADS // 2)]
        o_ref[...] = jnp.concatenate(pairs, axis=-1).astype(o_ref.dtype)

    per_class = KEY_CLASS // Q_BLOCK
    n_class = max(1, s_len // KEY_CLASS)
    for c in range(n_class):
        width = min(s_len, (c + 1) * KEY_CLASS)
        pl.when(i // per_class == c)(functools.partial(body, width))


def _sparse_prompt(z3, topk):
    bn, s_len, _ = z3.shape
    nq = s_len // Q_BLOCK
    return pl.pallas_call(
        functools.partial(_sparse_prompt_kernel, topk=topk, s_len=s_len),
        grid=(bn, nq),
        in_specs=[pl.BlockSpec((None, Q_BLOCK, 512), lambda b, i: (b, i, _blk("aq", 512))),
                  pl.BlockSpec((None, Q_BLOCK, 256), lambda b, i: (b, i, _blk("iq", 256))),
                  pl.BlockSpec((None, Q_BLOCK, 128), lambda b, i: (b, i, _blk("iw", 128))),
                  pl.BlockSpec((None, s_len, 256), lambda b, i: (b, 0, _blk("ak", 256))),
                  pl.BlockSpec((None, s_len, 256), lambda b, i: (b, 0, _blk("av", 256))),
                  pl.BlockSpec((None, s_len, 128), lambda b, i: (b, 0, _blk("ik", 128)))],
        out_specs=pl.BlockSpec((None, Q_BLOCK, A_WIDTH), lambda b, i: (b, i, 0)),
        out_shape=jax.ShapeDtypeStruct((bn, s_len, A_WIDTH), BF16),
        scratch_shapes=[pltpu.VMEM((A_KV_HEADS, s_len, LANES), BF16),
                        pltpu.VMEM((A_KV_HEADS, s_len, LANES), BF16),
                        pltpu.VMEM((s_len, LANES), BF16)],
        compiler_params=_cparams(("parallel", "arbitrary")),
        name="sparse_prompt",
    )(z3, z3, z3, z3, z3, z3)


PAD_ROWS = 8


def _bconv_kernel(gb_ref, gc_ref, xb_ref, w_ref, y_ref, tail_ref, pad_ref):
    tb = gb_ref.shape[0]

    @pl.when(pl.program_id(1) == 0)
    def _():
        pad_ref[0:PAD_ROWS, :] = jnp.zeros((PAD_ROWS, B_WIDTH), F32)

    p = gc_ref[...] * xb_ref[...]
    pad_ref[PAD_ROWS:PAD_ROWS + tb, :] = p
    w = w_ref[...]
    y = (pad_ref[PAD_ROWS - 2:PAD_ROWS - 2 + tb, :] * w[0:1]
         + pad_ref[PAD_ROWS - 1:PAD_ROWS - 1 + tb, :] * w[1:2]
         + p * w[2:3])
    y_ref[...] = (gb_ref[...] * y).astype(y_ref.dtype)
    tail_ref[...] = pad_ref[PAD_ROWS + tb - (B_CONV - 1):PAD_ROWS + tb, :]
    pad_ref[0:PAD_ROWS, :] = pad_ref[tb:tb + PAD_ROWS, :]


def _bconv_prompt(z3, conv_w):
    bn, s_len, _ = z3.shape
    tb = min(256, s_len)
    nt = s_len // tb
    spec = lambda name: pl.BlockSpec((None, tb, 512), lambda b, t: (b, t, _blk(name, 512)))
    return pl.pallas_call(
        _bconv_kernel,
        grid=(bn, nt),
        in_specs=[spec("gb"), spec("gc"), spec("xb"),
                  pl.BlockSpec((B_CONV, B_WIDTH), lambda b, t: (0, 0))],
        out_specs=[pl.BlockSpec((None, tb, B_WIDTH), lambda b, t: (b, t, 0)),
                   pl.BlockSpec((None, B_CONV - 1, B_WIDTH), lambda b, t: (b, 0, 0))],
        out_shape=[jax.ShapeDtypeStruct((bn, s_len, B_WIDTH), BF16),
                   jax.ShapeDtypeStruct((bn, B_CONV - 1, B_WIDTH), F32)],
        scratch_shapes=[pltpu.VMEM((PAD_ROWS + tb, B_WIDTH), F32)],
        compiler_params=_cparams(("parallel", "arbitrary")),
        name="bconv_prompt",
    )(z3, z3, z3, conv_w)


def _lru_coeffs(xc, wr, br, wi, bi, lam):
    xcb = xc.astype(BF16)
    r = jax.nn.sigmoid(_dot(xcb, wr) + br)
    ig = jax.nn.sigmoid(_dot(xcb, wi) + bi)
    nl = -lam
    softplus = jnp.maximum(nl, 0.0) + jnp.log1p(jnp.exp(-jnp.abs(nl)))
    log_a = -LRU_C * r * softplus
    a = jnp.exp(log_a)
    th = jnp.tanh(log_a)
    u = jnp.sqrt(-2.0 * th / (1.0 - th)) * (ig * xc)
    return a, u


def _rglru_kernel(dx_ref, dg_ref, cw_ref, cb_ref, wr_ref, br_ref, wi_ref, bi_ref, lam_ref,
                  y_ref, dtail_ref, hlast_ref, pad_ref, h_ref):
    tb = dx_ref.shape[0]

    @pl.when(pl.program_id(1) == 0)
    def _():
        pad_ref[0:PAD_ROWS, :] = jnp.zeros((PAD_ROWS, D_WIDTH), F32)
        h_ref[...] = jnp.zeros(h_ref.shape, F32)

    x = dx_ref[...]
    pad_ref[PAD_ROWS:PAD_ROWS + tb, :] = x
    cw = cw_ref[...]
    xc = pad_ref[PAD_ROWS - 3:PAD_ROWS - 3 + tb, :] * cw[0:1]
    xc = xc + pad_ref[PAD_ROWS - 2:PAD_ROWS - 2 + tb, :] * cw[1:2]
    xc = xc + pad_ref[PAD_ROWS - 1:PAD_ROWS - 1 + tb, :] * cw[2:3]
    xc = xc + x * cw[3:4]
    xc = xc + cb_ref[...]
    a, u = _lru_coeffs(xc, wr_ref[...], br_ref[...], wi_ref[...], bi_ref[...], lam_ref[...])
    row = lax.broadcasted_iota(jnp.int32, a.shape, 0)
    d = 1
    while d < tb:
        keep = row >= d
        a_prev = jnp.where(keep, pltpu.roll(a, d, 0), 1.0)
        u_prev = jnp.where(keep, pltpu.roll(u, d, 0), 0.0)
        u = a * u_prev + u
        a = a * a_prev
        d *= 2
    h = a * h_ref[0:1, :] + u
    y_ref[...] = (jax.nn.gelu(dg_ref[...]) * h).astype(y_ref.dtype)
    h_ref[0:1, :] = h[tb - 1:tb, :]
    hlast_ref[...] = h[tb - 1:tb, :]
    dtail_ref[...] = pad_ref[PAD_ROWS + tb - (D_CONV - 1):PAD_ROWS + tb, :]
    pad_ref[0:PAD_ROWS, :] = pad_ref[tb:tb + PAD_ROWS, :]


def _rglru_prompt(z3, cw, cb, wr, br, wi, bi, lam):
    bn, s_len, _ = z3.shape
    tb = min(256, s_len)
    nt = s_len // tb
    spec = lambda name: pl.BlockSpec((None, tb, 512), lambda b, t: (b, t, _blk(name, 512)))
    vec = pl.BlockSpec((1, D_WIDTH), lambda b, t: (0, 0))
    mat = pl.BlockSpec((D_WIDTH, D_WIDTH), lambda b, t: (0, 0))
    return pl.pallas_call(
        _rglru_kernel,
        grid=(bn, nt),
        in_specs=[spec("dx"), spec("dg"), pl.BlockSpec((D_CONV, D_WIDTH), lambda b, t: (0, 0)),
                  vec, mat, vec, mat, vec, vec],
        out_specs=[pl.BlockSpec((None, tb, D_WIDTH), lambda b, t: (b, t, 0)),
                   pl.BlockSpec((None, D_CONV - 1, D_WIDTH), lambda b, t: (b, 0, 0)),
                   pl.BlockSpec((None, 1, D_WIDTH), lambda b, t: (b, 0, 0))],
        out_shape=[jax.ShapeDtypeStruct((bn, s_len, D_WIDTH), BF16),
                   jax.ShapeDtypeStruct((bn, D_CONV - 1, D_WIDTH), F32),
                   jax.ShapeDtypeStruct((bn, 1, D_WIDTH), F32)],
        scratch_shapes=[pltpu.VMEM((PAD_ROWS + tb, D_WIDTH), F32),
                        pltpu.VMEM((8, D_WIDTH), F32)],
        compiler_params=_cparams(("parallel", "arbitrary")),
        name="rglru_prompt",
    )(z3, z3, cw, cb, wr, br, wi, bi, lam)


def _sample_bd_kernel(gb_ref, gc_ref, xb_ref, dg_ref, dx_ref, sb_ref, sd_ref, h0_ref,
                      bw_ref, cw_ref, cb_ref, wr_ref, br_ref, wi_ref, bi_ref, lam_ref,
                      yb_ref, yd_ref, bnew_ref, dnew_ref, hnew_ref):
    w = B_WIDTH
    p = gc_ref[...] * xb_ref[...]
    bw = bw_ref[...]
    b0 = sb_ref[:, 0:w]
    b1 = sb_ref[:, w:2 * w]
    y = b0 * bw[0:1] + b1 * bw[1:2] + p * bw[2:3]
    yb_ref[...] = (gb_ref[...] * y).astype(yb_ref.dtype)
    bnew_ref[:, 0:w] = b1
    bnew_ref[:, w:2 * w] = p

    w = D_WIDTH
    x = dx_ref[...]
    cw = cw_ref[...]
    d0 = sd_ref[:, 0:w]
    d1 = sd_ref[:, w:2 * w]
    d2 = sd_ref[:, 2 * w:3 * w]
    xc = d0 * cw[0:1]
    xc = xc + d1 * cw[1:2]
    xc = xc + d2 * cw[2:3]
    xc = xc + x * cw[3:4]
    xc = xc + cb_ref[...]
    a, u = _lru_coeffs(xc, wr_ref[...], br_ref[...], wi_ref[...], bi_ref[...], lam_ref[...])
    h = a * h0_ref[...] + u
    hnew_ref[...] = h
    yd_ref[...] = (jax.nn.gelu(dg_ref[...]) * h).astype(yd_ref.dtype)
    dnew_ref[:, 0:w] = d1
    dnew_ref[:, w:2 * w] = d2
    dnew_ref[:, 2 * w:3 * w] = x


def _sample_bd(z, sb, sd, h0, bw, cw, cb, wr, br, wi, bi, lam):
    m = z.shape[0]
    zspec = lambda name: pl.BlockSpec((m, 512), lambda i: (0, _blk(name, 512)))
    full = lambda a: pl.BlockSpec(a.shape, lambda i: (0,) * a.ndim)
    args = (sb, sd, h0, bw, cw, cb, wr, br, wi, bi, lam)
    out_shapes = [jax.ShapeDtypeStruct((m, B_WIDTH), BF16),
                  jax.ShapeDtypeStruct((m, D_WIDTH), BF16),
                  jax.ShapeDtypeStruct((m, (B_CONV - 1) * B_WIDTH), F32),
                  jax.ShapeDtypeStruct((m, (D_CONV - 1) * D_WIDTH), F32),
                  jax.ShapeDtypeStruct((m, D_WIDTH), F32)]
    return pl.pallas_call(
        _sample_bd_kernel,
        grid=(1,),
        in_specs=[zspec("gb"), zspec("gc"), zspec("xb"), zspec("dg"), zspec("dx")]
                 + [full(a) for a in args],
        out_specs=[pl.BlockSpec(s.shape, lambda i: (0, 0)) for s in out_shapes],
        out_shape=out_shapes,
        compiler_params=_cparams(("arbitrary",)),
        name="sample_bd",
    )(z, z, z, z, z, *args)


def _merge_kernel(ya_ref, yb_ref, yc_ref, yd_ref, g0_ref, g1_ref, g2_ref, g3_ref, x_ref,
                  wa_ref, wb_ref, wc_ref, wd_ref, wo_ref, gp_ref, o_ref):
    acc = jax.nn.sigmoid(g0_ref[...]) * _dot(ya_ref[...], wa_ref[...])
    acc = acc + jax.nn.sigmoid(g1_ref[...]) * _dot(yb_ref[...], wb_ref[...])
    acc = acc + jax.nn.sigmoid(g2_ref[...]) * _dot(yc_ref[...], wc_ref[...])
    acc = acc + jax.nn.sigmoid(g3_ref[...]) * _dot(yd_ref[...], wd_ref[...])
    m = _dot(acc.astype(BF16), wo_ref[...])
    o_ref[...] = x_ref[...] + _rms(m, gp_ref[...])


def _merge(ya, yb, yc, yd, z, x, wa, wb, wc, wd, wo, gp, tm):
    m = x.shape[0]
    tm = min(tm, m)
    row = lambda w: pl.BlockSpec((tm, w), lambda i: (i, 0))
    gate = lambda k: pl.BlockSpec((tm, D_MODEL), lambda i: (i, _blk("gt", D_MODEL) + k))
    const = lambda a: pl.BlockSpec(a.shape, lambda i: (0, 0))
    return pl.pallas_call(
        _merge_kernel,
        grid=(m // tm,),
        in_specs=[row(512), row(512), row(512), row(512), gate(0), gate(1), gate(2), gate(3),
                  row(D_MODEL), const(wa), const(wb), const(wc), const(wd), const(wo), const(gp)],
        out_specs=row(D_MODEL),
        out_shape=jax.ShapeDtypeStruct((m, D_MODEL), F32),
        compiler_params=_cparams(("parallel",)),
        name="merge",
    )(ya, yb, yc, yd, z, z, z, z, x, wa, wb, wc, wd, wo, gp)


def _ffn_kernel(x_ref, gpre_ref, wg_ref, wu_ref, wd_ref, gpost_ref, o_ref, h_ref, acc_ref, *, nf):
    j = pl.program_id(1)

    @pl.when(j == 0)
    def _():
        h_ref[...] = _rms(x_ref[...], gpre_ref[...]).astype(BF16)
        acc_ref[...] = jnp.zeros(acc_ref.shape, F32)

    h = h_ref[...]
    act = jax.nn.silu(_dot(h, wg_ref[...])) * _dot(h, wu_ref[...])
    acc_ref[...] += _dot(act.astype(BF16), wd_ref[...])

    @pl.when(j == nf - 1)
    def _():
        o_ref[...] = x_ref[...] + _rms(acc_ref[...], gpost_ref[...])


def _ffn(x, gpre, wg, wu, wd, gpost, tm, tf=256):
    m = x.shape[0]
    tm = min(tm, m)
    nf = D_FF // tf
    return pl.pallas_call(
        functools.partial(_ffn_kernel, nf=nf),
        grid=(m // tm, nf),
        in_specs=[pl.BlockSpec((tm, D_MODEL), lambda i, j: (i, 0)),
                  pl.BlockSpec((1, D_MODEL), lambda i, j: (0, 0)),
                  pl.BlockSpec((D_MODEL, tf), lambda i, j: (0, j)),
                  pl.BlockSpec((D_MODEL, tf), lambda i, j: (0, j)),
                  pl.BlockSpec((tf, D_MODEL), lambda i, j: (j, 0)),
                  pl.BlockSpec((1, D_MODEL), lambda i, j: (0, 0))],
        out_specs=pl.BlockSpec((tm, D_MODEL), lambda i, j: (i, 0)),
        out_shape=jax.ShapeDtypeStruct((m, D_MODEL), F32),
        scratch_shapes=[pltpu.VMEM((tm, D_MODEL), BF16), pltpu.VMEM((tm, D_MODEL), F32)],
        compiler_params=_cparams(("parallel", "arbitrary")),
        name="ffn",
    )(x, gpre, wg, wu, wd, gpost)


IDX_ROWS = 8
PAGES_PER_SCORE_STEP = 32
PAGES_PER_ATTN_STEP = 32


def _sample_scores_kernel(pt_ref, iq_ref, iw_ref, ikn_ref, *rest, g_pages, n_pages):
    ki_refs = rest[:g_pages]
    o_ref = rest[g_pages]
    j = pl.program_id(1)
    qi = iq_ref[...].astype(BF16)
    wi = iw_ref[...]
    rel = jnp.concatenate([_dot(qi, ki_refs[g][...].astype(BF16)) for g in range(g_pages)], axis=-1)
    sc = jnp.sum(wi * jnp.maximum(rel, 0.0), axis=0, keepdims=True)
    start = pl.multiple_of(j * (g_pages * PAGE_SIZE), g_pages * PAGE_SIZE)
    o_ref[:, pl.ds(start, g_pages * PAGE_SIZE)] = sc

    @pl.when(j == n_pages // g_pages - 1)
    def _():
        kn = ikn_ref[...].astype(BF16).astype(F32)
        rel = jnp.maximum(jnp.sum(qi.astype(F32) * kn, axis=1, keepdims=True), 0.0)
        sc = jnp.sum(wi * rel, axis=0, keepdims=True)
        lane = lax.broadcasted_iota(jnp.int32, (1, PAGE_SIZE), 1)
        o_ref[:, n_pages * PAGE_SIZE:(n_pages + 1) * PAGE_SIZE] = jnp.where(lane == 0, sc, -jnp.inf)


def _sample_scores(pt_flat, iq3, iw3, ikn3, cache_ki, layer, n_pages, g_pages):
    bd = iq3.shape[0]
    ns = (n_pages + 1) * PAGE_SIZE

    def page_map(g):
        return lambda b, j, pt: (layer, pt[b * n_pages + j * g_pages + g], 0, 0)

    grid_spec = pltpu.PrefetchScalarGridSpec(
        num_scalar_prefetch=1,
        grid=(bd, n_pages // g_pages),
        in_specs=[pl.BlockSpec((None, IDX_ROWS, IDX_DIM), lambda b, j, pt: (b, 0, 0)),
                  pl.BlockSpec((None, IDX_ROWS, 1), lambda b, j, pt: (b, 0, 0)),
                  pl.BlockSpec((None, 1, IDX_DIM), lambda b, j, pt: (b, 0, 0))]
                 + [pl.BlockSpec((None, None, IDX_DIM, PAGE_SIZE), page_map(g)) for g in range(g_pages)],
        out_specs=pl.BlockSpec((None, 1, ns), lambda b, j, pt: (b, 0, 0)),
    )
    return pl.pallas_call(
        functools.partial(_sample_scores_kernel, g_pages=g_pages, n_pages=n_pages),
        grid_spec=grid_spec,
        out_shape=jax.ShapeDtypeStruct((bd, 1, ns), F32),
        compiler_params=_cparams(("parallel", "arbitrary")),
        name="sample_scores",
    )(pt_flat, iq3, iw3, ikn3, *([cache_ki] * g_pages))


def _sample_select_kernel(s_ref, o_ref, *, topk, nbits, n_valid):
    xs0 = s_ref[...]
    col = lax.broadcasted_iota(jnp.int32, xs0.shape, 1)
    allowed = col < n_valid
    xs = jnp.where(allowed, xs0 + 0.0, -jnp.inf)
    sel = _topk_select(xs, allowed, col, topk, nbits, N_BISECT_SAMPLE)
    o_ref[...] = jnp.where(sel, 1.0, 0.0)


def _sample_select(scores, topk, n_valid):
    bd, ns = scores.shape
    nbits = max(1, (ns - 1).bit_length())
    return pl.pallas_call(
        functools.partial(_sample_select_kernel, topk=topk, nbits=nbits, n_valid=n_valid),
        grid=(1,),
        in_specs=[pl.BlockSpec((bd, ns), lambda i: (0, 0))],
        out_specs=pl.BlockSpec((bd, ns), lambda i: (0, 0)),
        out_shape=jax.ShapeDtypeStruct((bd, ns), F32),
        compiler_params=_cparams(("arbitrary",)),
        name="sample_select",
    )(scores)


def _online_update(s, weighted_values, m_ref, l_ref, acc_ref):
    m_old = m_ref[...]
    m_new = jnp.maximum(m_old, jnp.max(s, axis=1, keepdims=True))
    alpha = jnp.exp(m_old - m_new)
    p = jnp.exp(s - m_new)
    l_ref[...] = alpha * l_ref[...] + _rowsum(p)
    acc_ref[...] = alpha * acc_ref[...] + weighted_values(p.astype(BF16))
    m_ref[...] = m_new


def _online_update_one(s, v_row, m_ref, l_ref, acc_ref):
    m_old = m_ref[...]
    m_new = jnp.maximum(m_old, s)
    alpha = jnp.exp(m_old - m_new)
    p = jnp.exp(s - m_new)
    l_ref[...] = alpha * l_ref[...] + p
    acc_ref[...] = alpha * acc_ref[...] + p.astype(BF16).astype(F32) * v_row
    m_ref[...] = m_new


def _sample_attn_kernel(pt_ref, mask_ref, qa_ref, ql_ref, qr_ref, kn_ref, vn_ref, cn_ref, rn_ref,
                        *rest, g_pages, n_pages):
    k_refs = rest[0:g_pages]
    v_refs = rest[g_pages:2 * g_pages]
    c_refs = rest[2 * g_pages:3 * g_pages]
    r_refs = rest[3 * g_pages:4 * g_pages]
    oa_ref, oc_ref, ma_ref, la_ref, acca_ref, mc_ref, lc_ref, accc_ref = rest[4 * g_pages:]
    j = pl.program_id(1)

    @pl.when(j == 0)
    def _():
        for m_ref, l_ref, acc_ref in ((ma_ref, la_ref, acca_ref), (mc_ref, lc_ref, accc_ref)):
            m_ref[...] = jnp.full(m_ref.shape, NEG_BIG, F32)
            l_ref[...] = jnp.zeros(l_ref.shape, F32)
            acc_ref[...] = jnp.zeros(acc_ref.shape, F32)

    qa = qa_ref[...].astype(BF16)
    ql = ql_ref[...]
    qr = qr_ref[...].astype(BF16)
    span = g_pages * PAGE_SIZE
    start = pl.multiple_of(j * span, span)
    page = lambda x, g: x[:, g * PAGE_SIZE:(g + 1) * PAGE_SIZE]

    member = mask_ref[:, pl.ds(start, span)] > 0.5
    s = jnp.concatenate([_dot(qa, k_refs[g][...].astype(BF16)) for g in range(g_pages)], axis=-1)
    s = jnp.where(member, s * A_SCALE, -jnp.inf)
    _online_update(s, lambda p: sum(_dot_nt(page(p, g), v_refs[g][...].astype(BF16))
                                    for g in range(g_pages)), ma_ref, la_ref, acca_ref)

    cks = [c_refs[g][...].astype(BF16) for g in range(g_pages)]
    s = jnp.concatenate([_dot_nt(ql, cks[g]) + _dot(qr, r_refs[g][...].astype(BF16))
                         for g in range(g_pages)], axis=-1) * C_SCALE
    _online_update(s, lambda p: sum(_dot(page(p, g), cks[g]) for g in range(g_pages)),
                   mc_ref, lc_ref, accc_ref)

    @pl.when(j == n_pages // g_pages - 1)
    def _():
        rnd = lambda x: x.astype(BF16).astype(F32)
        member = mask_ref[:, n_pages * PAGE_SIZE:n_pages * PAGE_SIZE + 1] > 0.5
        s = jnp.sum(rnd(qa_ref[...]) * rnd(kn_ref[...]), axis=1, keepdims=True) * A_SCALE
        s = jnp.where(member, s, -jnp.inf)
        _online_update_one(s, rnd(vn_ref[...]), ma_ref, la_ref, acca_ref)
        cn = rnd(cn_ref[...])
        s = (jnp.sum(ql.astype(F32) * cn, axis=1, keepdims=True)
             + jnp.sum(rnd(qr_ref[...]) * rnd(rn_ref[...]), axis=1, keepdims=True)) * C_SCALE
        _online_update_one(s, cn, mc_ref, lc_ref, accc_ref)
        o = acca_ref[...] / la_ref[...]
        first_group = lax.broadcasted_iota(jnp.int32, oa_ref.shape, 0) < A_HEADS // A_KV_HEADS
        oa_ref[...] = jnp.where(first_group, o[:, :A_HEAD_DIM], o[:, A_HEAD_DIM:])
        oc_ref[...] = accc_ref[...] / lc_ref[...]


def _sample_attn(pt_flat, mask3, qa, ql, qr, kn, vn, cn, rn, cache_k, cache_v, cache_c, cache_r,
                 layer, n_pages, g_pages):
    bd = qa.shape[0]
    ns = mask3.shape[-1]

    def page_map(g):
        return lambda b, j, pt: (layer, pt[b * n_pages + j * g_pages + g], 0, 0)

    per_b = lambda r, w: pl.BlockSpec((None, r, w), lambda b, j, pt: (b, 0, 0))
    pages = lambda r, w: [pl.BlockSpec((None, None, r, w), page_map(g)) for g in range(g_pages)]
    grid_spec = pltpu.PrefetchScalarGridSpec(
        num_scalar_prefetch=1,
        grid=(bd, n_pages // g_pages),
        in_specs=[per_b(1, ns), per_b(A_HEADS, LANES), per_b(C_HEADS, C_KV_RANK), per_b(C_HEADS, C_ROPE),
                  per_b(1, LANES), per_b(1, LANES), per_b(1, C_KV_RANK), per_b(1, C_ROPE)]
                 + pages(A_KV_HEADS * A_HEAD_DIM, PAGE_SIZE) + pages(A_KV_HEADS * A_HEAD_DIM, PAGE_SIZE)
                 + pages(PAGE_SIZE, C_KV_RANK) + pages(C_ROPE, PAGE_SIZE),
        out_specs=[per_b(A_HEADS, A_HEAD_DIM), per_b(C_HEADS, C_KV_RANK)],
        scratch_shapes=[pltpu.VMEM((A_HEADS, 1), F32), pltpu.VMEM((A_HEADS, 1), F32),
                        pltpu.VMEM((A_HEADS, LANES), F32),
                        pltpu.VMEM((C_HEADS, 1), F32), pltpu.VMEM((C_HEADS, 1), F32),
                        pltpu.VMEM((C_HEADS, C_KV_RANK), F32)],
    )
    return pl.pallas_call(
        functools.partial(_sample_attn_kernel, g_pages=g_pages, n_pages=n_pages),
        grid_spec=grid_spec,
        out_shape=[jax.ShapeDtypeStruct((bd, A_HEADS, A_HEAD_DIM), F32),
                   jax.ShapeDtypeStruct((bd, C_HEADS, C_KV_RANK), F32)],
        compiler_params=_cparams(("parallel", "arbitrary")),
        name="sample_attn",
    )(pt_flat, mask3, qa, ql, qr, kn, vn, cn, rn,
      *([cache_k] * g_pages), *([cache_v] * g_pages), *([cache_c] * g_pages), *([cache_r] * g_pages))


def _uv_kernel(o_ref, wuv_ref, y_ref):
    w = 2 * C_KV_RANK
    outs = [_dot(o_ref[:, w * p:w * (p + 1)].astype(BF16), wuv_ref[p]) for p in range(C_HEADS // 2)]
    y_ref[...] = jnp.concatenate(outs, axis=-1).astype(y_ref.dtype)


def _uv_proj(o_lat, wuvp):
    m = o_lat.shape[0]
    return pl.pallas_call(
        _uv_kernel,
        grid=(1,),
        in_specs=[pl.BlockSpec(o_lat.shape, lambda i: (0, 0)),
                  pl.BlockSpec(wuvp.shape, lambda i: (0, 0, 0))],
        out_specs=pl.BlockSpec((m, C_WIDTH), lambda i: (0, 0)),
        out_shape=jax.ShapeDtypeStruct((m, C_WIDTH), BF16),
        compiler_params=_cparams(("arbitrary",)),
        name="uv_proj",
    )(o_lat, wuvp)


def _rope_tables(pos):
    half = C_ROPE // 2
    freqs = ROPE_THETA ** (-jnp.arange(half, dtype=F32) / half)
    ang = pos.astype(F32)[:, None] * freqs[None, :]
    return jnp.tile(jnp.cos(ang), (1, C_HEADS)), jnp.tile(jnp.sin(ang), (1, C_HEADS))


def _layer_weights(l, w_in, w_out_a, conv_b, w_out_b, g_cq, w_uq, g_ckv, w_uk, w_uv, w_out_c,
                   conv_d_w, conv_d_b, w_rg, b_rg, w_ig, b_ig, lru_lambda, w_out_d, w_o,
                   w_ffn_gate, w_ffn_up, w_ffn_down):
    lw = {}
    lw["w_in"] = _relayout_w_in(jnp.transpose(w_in, (0, 2, 1))[l])
    uq = w_uq[l].reshape(C_Q_RANK, C_HEADS, C_NOPE + C_ROPE)
    half = C_ROPE // 2
    lw["w_uq"] = jnp.concatenate(
        [uq[:, :, :C_NOPE].reshape(C_Q_RANK, -1),
         uq[:, :, C_NOPE:C_NOPE + half].reshape(C_Q_RANK, -1),
         uq[:, :, C_NOPE + half:].reshape(C_Q_RANK, -1)], axis=1).astype(BF16)
    ukt = jnp.transpose(w_uk[l], (1, 2, 0))
    zk = jnp.zeros((C_NOPE, C_KV_RANK), F32)
    lw["w_uk"] = jnp.stack([
        jnp.concatenate([jnp.concatenate([ukt[2 * p], zk], axis=1),
                         jnp.concatenate([zk, ukt[2 * p + 1]], axis=1)], axis=0)
        for p in range(C_HEADS // 2)]).astype(BF16)
    uv = jnp.transpose(w_uv[l], (1, 0, 2))
    zv = jnp.zeros((C_KV_RANK, C_VDIM), F32)
    lw["w_uv"] = jnp.stack([
        jnp.concatenate([jnp.concatenate([uv[2 * p], zv], axis=1),
                         jnp.concatenate([zv, uv[2 * p + 1]], axis=1)], axis=0)
        for p in range(C_HEADS // 2)]).astype(BF16)
    eye = jnp.eye(D_BLOCKS, dtype=F32)
    bd = lambda w: (eye[:, None, :, None] * w[:, :, None, :]).reshape(D_WIDTH, D_WIDTH).astype(BF16)
    lw["w_rg"] = bd(w_rg[l])
    lw["w_ig"] = bd(w_ig[l])
    row = lambda v: v[l][None, :]
    lw["b_rg"], lw["b_ig"], lw["lam"], lw["conv_d_b"] = row(b_rg), row(b_ig), row(lru_lambda), row(conv_d_b)
    lw["g_cq"], lw["g_ckv"] = row(g_cq), row(g_ckv)
    lw["conv_b"], lw["conv_d_w"] = conv_b[l], conv_d_w[l]
    for name, w in (("w_out_a", w_out_a), ("w_out_b", w_out_b), ("w_out_c", w_out_c),
                    ("w_out_d", w_out_d), ("w_o", w_o), ("w_ffn_gate", w_ffn_gate),
                    ("w_ffn_up", w_ffn_up), ("w_ffn_down", w_ffn_down)):
        lw[name] = w[l].astype(BF16)
    return lw


def _zcols(z, name, lo, hi):
    o = _DST[name][0]
    return z[..., o + lo:o + hi]


def _state_from_z(z):
    hd = A_HEAD_DIM
    ak = jnp.concatenate([_zcols(z, "ak", 0, hd), _zcols(z, "ak", 2 * hd, 3 * hd)], axis=-1)
    av = jnp.concatenate([_zcols(z, "av", 0, hd), _zcols(z, "av", 2 * hd, 3 * hd)], axis=-1)
    ik = _zcols(z, "ik", 0, IDX_DIM)
    return ak, av, ik


def _krope_state(kr):
    half = C_ROPE // 2
    return jnp.concatenate([kr[:, :half], kr[:, LANES:LANES + half]], axis=-1)


def _prompt_layer(x, lw, gains, bn, s_len, cos, sin):
    g_mix_pre, g_mix_post, g_ffn_pre, g_ffn_post = gains
    m = bn * s_len
    z = _inproj(x, g_mix_pre, lw["w_in"], tm=1024)
    z3 = z.reshape(bn, s_len, N_PROJ)
    tm_c = min(512, s_len)
    nb = s_len // tm_c
    q, kcat, ckvn, kr, _ = _cprep(z, cos, sin, lambda i: (i % nb, 0), lw["g_cq"], lw["g_ckv"],
                                  lw["w_uq"], lw["w_uk"], tm=tm_c)
    yc = _mla_prompt(q, kcat, lw["w_uv"], bn, s_len)
    ya = _sparse_prompt(z3, min(TOPK_MAX, s_len // 4)).reshape(m, A_WIDTH)
    yb, b_new = _bconv_prompt(z3, lw["conv_b"])
    yd, d_new, h_new = _rglru_prompt(z3, lw["conv_d_w"], lw["conv_d_b"], lw["w_rg"], lw["b_rg"],
                                     lw["w_ig"], lw["b_ig"], lw["lam"])
    x = _merge(ya, yb.reshape(m, B_WIDTH), yc, yd.reshape(m, D_WIDTH), z, x,
               lw["w_out_a"], lw["w_out_b"], lw["w_out_c"], lw["w_out_d"], lw["w_o"], g_mix_post, tm=512)
    x = _ffn(x, g_ffn_pre, lw["w_ffn_gate"], lw["w_ffn_up"], lw["w_ffn_down"], g_ffn_post, tm=1024)
    ak, av, ik = _state_from_z(z3)
    st = (ak.reshape(bn, s_len, A_KV_HEADS, A_HEAD_DIM), av.reshape(bn, s_len, A_KV_HEADS, A_HEAD_DIM),
          ik, ckvn.reshape(bn, s_len, C_KV_RANK), _krope_state(kr).reshape(bn, s_len, C_ROPE),
          b_new, d_new, h_new.reshape(bn, D_WIDTH))
    return x, st


def _sample_layer(x, lw, gains, layer, caches, states, pt_flat, n_pages, cos, sin, g_pages):
    g_mix_pre, g_mix_post, g_ffn_pre, g_ffn_post = gains
    cache_k, cache_v, cache_ki, cache_c, cache_r = caches
    sb, sd, h0 = states
    bd = x.shape[0]
    z = _inproj(x, g_mix_pre, lw["w_in"], tm=bd)
    q, _, ckvn, kr, qrope = _cprep(z, cos, sin, lambda i: (0, 0), lw["g_cq"], lw["g_ckv"],
                                   lw["w_uq"], lw["w_uk"], tm=bd)
    ak, av, ik = _state_from_z(z)
    krope = _krope_state(kr)

    iq3 = jnp.pad(_zcols(z, "iq", 0, 256).reshape(bd, IDX_HEADS, IDX_DIM),
                  ((0, 0), (0, IDX_ROWS - IDX_HEADS), (0, 0)))
    iw3 = jnp.pad(_zcols(z, "iw", 0, IDX_HEADS), ((0, 0), (0, IDX_ROWS - IDX_HEADS)))[:, :, None]
    scores = _sample_scores(pt_flat, iq3, iw3, ik[:, None, :], cache_ki, layer, n_pages, g_pages[0])
    total = n_pages * PAGE_SIZE + 1
    mask = _sample_select(scores.reshape(bd, -1), min(TOPK_MAX, total // 4), total)

    aq = _zcols(z, "aq", 0, A_WIDTH).reshape(bd, A_KV_HEADS, A_HEADS // A_KV_HEADS, A_HEAD_DIM)
    zq = jnp.zeros_like(aq[:, 0])
    qa = jnp.concatenate([jnp.concatenate([aq[:, 0], zq], axis=-1),
                          jnp.concatenate([zq, aq[:, 1]], axis=-1)], axis=1)
    ql = jnp.transpose(q[:, :, :C_KV_RANK], (1, 0, 2))
    half = C_ROPE // 2
    qr = jnp.concatenate([qrope[:, :LANES].reshape(bd, C_HEADS, half),
                          qrope[:, LANES:].reshape(bd, C_HEADS, half)], axis=-1)
    oa, oc = _sample_attn(pt_flat, mask[:, None, :], qa, ql, qr, ak[:, None, :], av[:, None, :],
                          ckvn[:, None, :], krope[:, None, :], cache_k, cache_v, cache_c, cache_r,
                          layer, n_pages, g_pages[1])
    ya = oa.reshape(bd, A_WIDTH).astype(BF16)
    yc = _uv_proj(oc.reshape(bd, C_HEADS * C_KV_RANK), lw["w_uv"])

    yb, yd, b_new, d_new, h_new = _sample_bd(
        z, sb.reshape(bd, -1), sd.reshape(bd, -1), h0, lw["conv_b"], lw["conv_d_w"], lw["conv_d_b"],
        lw["w_rg"], lw["b_rg"], lw["w_ig"], lw["b_ig"], lw["lam"])
    x = _merge(ya, yb, yc, yd, z, x, lw["w_out_a"], lw["w_out_b"], lw["w_out_c"], lw["w_out_d"],
               lw["w_o"], g_mix_post, tm=bd)
    x = _ffn(x, g_ffn_pre, lw["w_ffn_gate"], lw["w_ffn_up"], lw["w_ffn_down"], g_ffn_post, tm=bd)
    st = (ak.reshape(bd, 1, A_KV_HEADS, A_HEAD_DIM), av.reshape(bd, 1, A_KV_HEADS, A_HEAD_DIM),
          ik[:, None, :], ckvn[:, None, :], krope[:, None, :],
          b_new.reshape(bd, B_CONV - 1, B_WIDTH), d_new.reshape(bd, D_CONV - 1, D_WIDTH), h_new)
    return x, st


def kernel(x_prompt, x_sample, cache_a_k, cache_a_v, cache_a_kidx, cache_c_kv, cache_c_krope, state_b_conv, state_d_conv, state_d_h, page_table, g_mix_pre, g_mix_post, g_ffn_pre, g_ffn_post, w_in, w_out_a, conv_b, w_out_b, g_cq, w_uq, g_ckv, w_uk, w_uv, w_out_c, conv_d_w, conv_d_b, w_rg, b_rg, w_ig, b_ig, lru_lambda, w_out_d, w_o, w_ffn_gate, w_ffn_up, w_ffn_down):
    bn, s_len, _ = x_prompt.shape
    bd, t_len, _ = x_sample.shape
    assert t_len == 1 and s_len % Q_BLOCK == 0
    depth = w_in.shape[0]
    n_pages = page_table.shape[1]
    n_pool = cache_a_k.shape[1]
    past = n_pages * PAGE_SIZE
    largest_group = lambda cap: max(g for g in range(1, cap + 1) if n_pages % g == 0)
    g_pages = (largest_group(PAGES_PER_SCORE_STEP), largest_group(PAGES_PER_ATTN_STEP))

    cos_p, sin_p = _rope_tables(jnp.arange(s_len))
    cos_s, sin_s = _rope_tables(jnp.full((bd,), past))
    pt_flat = page_table.reshape(-1)
    kv_t = lambda c: jnp.transpose(c, (0, 1, 3, 4, 2)).reshape(
        depth, n_pool, A_KV_HEADS * A_HEAD_DIM, PAGE_SIZE)
    caches = (kv_t(cache_a_k), kv_t(cache_a_v), jnp.transpose(cache_a_kidx, (0, 1, 3, 2)),
              cache_c_kv, jnp.transpose(cache_c_krope, (0, 1, 3, 2)))

    xp = x_prompt.reshape(bn * s_len, D_MODEL)
    xs = x_sample.reshape(bd, D_MODEL)
    p_states, s_states = [], []
    for l in range(depth):
        lw = _layer_weights(l, w_in, w_out_a, conv_b, w_out_b, g_cq, w_uq, g_ckv, w_uk, w_uv, w_out_c,
                            conv_d_w, conv_d_b, w_rg, b_rg, w_ig, b_ig, lru_lambda, w_out_d, w_o,
                            w_ffn_gate, w_ffn_up, w_ffn_down)
        gains = tuple(g[l][None, :] for g in (g_mix_pre, g_mix_post, g_ffn_pre, g_ffn_post))
        xp, st_p = _prompt_layer(xp, lw, gains, bn, s_len, cos_p, sin_p)
        xs, st_s = _sample_layer(xs, lw, gains, l, caches,
                                 (state_b_conv[l], state_d_conv[l], state_d_h[l]),
                                 pt_flat, n_pages, cos_s, sin_s, g_pages)
        p_states.append(st_p)
        s_states.append(st_s)
    p_out = [jnp.stack([st[j] for st in p_states]) for j in range(8)]
    s_out = [jnp.stack([st[j] for st in s_states]) for j in range(8)]
    return (xp.reshape(bn, s_len, D_MODEL), xs.reshape(bd, 1, D_MODEL), *p_out, *s_out)
```

```python
import functools

import numpy as np
import jax
import jax.numpy as jnp
from jax import lax
from jax.experimental import pallas as pl
from jax.experimental.pallas import tpu as pltpu

D_MODEL = 1024
PAGE_SIZE = 128
A_HEADS = 8
A_KV_HEADS = 2
A_HEAD_DIM = 64
A_WIDTH = A_HEADS * A_HEAD_DIM
A_SCALE = A_HEAD_DIM ** -0.5
IDX_HEADS = 4
IDX_DIM = 64
TOPK_MAX = 256
B_WIDTH = 512
B_CONV = 3
C_HEADS = 8
C_NOPE = 64
C_ROPE = 32
C_VDIM = 64
C_Q_RANK = 256
C_KV_RANK = 256
C_WIDTH = C_HEADS * C_VDIM
C_SCALE = (C_NOPE + C_ROPE) ** -0.5
ROPE_THETA = 10000.0
D_WIDTH = 512
D_BLOCKS = 8
D_BLOCK = D_WIDTH // D_BLOCKS
D_CONV = 4
LRU_C = 8.0
N_BRANCH = 4
D_FF = ((8 * D_MODEL + 3 * 256 - 1) // (3 * 256)) * 256
Q_BLOCK = 128
EPS = 1e-6

LANES = 128
VMEM_LIMIT = 56 * 1024 * 1024
NEG_BIG = -1e30

BF16 = jnp.bfloat16
F32 = jnp.float32

_SRC = {}
_o = 0
for _n, _w in (("aq", A_WIDTH), ("ak", 128), ("av", 128), ("iq", 256), ("ik", 64), ("iw", 4),
               ("gb", 512), ("gc", 512), ("xb", 512), ("cq", 256), ("ckv", 256), ("ckr", 32),
               ("dg", 512), ("dx", 512), ("gt", 4096)):
    _SRC[_n] = (_o, _w)
    _o += _w
N_IN = _o

_DST = {}
_o = 0
for _n, _w in (("aq", 512), ("gb", 512), ("gc", 512), ("xb", 512), ("dg", 512), ("dx", 512),
               ("gt", 4096), ("iq", 256), ("cq", 256), ("ckv", 256), ("ckr", 256),
               ("ak", 256), ("av", 256), ("ik", 128), ("iw", 128)):
    _DST[_n] = (_o, _w)
    _o += _w
N_PROJ = _o
PROJ_TN = 1280


def _blk(name, width):
    off = _DST[name][0]
    assert off % width == 0
    return off // width


def _cparams(sem, vmem=VMEM_LIMIT):
    return pltpu.CompilerParams(dimension_semantics=sem, vmem_limit_bytes=vmem)


def _relayout_w_in(wt):
    def src(name, lo=0, hi=None):
        o, wd = _SRC[name]
        hi = wd if hi is None else hi
        return wt[o + lo:o + hi]

    zeros = lambda n: jnp.zeros((n, wt.shape[1]), wt.dtype)
    parts = [src("aq"), src("gb"), src("gc"), src("xb"), src("dg"), src("dx"), src("gt"),
             src("iq"), src("cq"), src("ckv"),
             jnp.tile(src("ckr", 0, 16), (8, 1)), jnp.tile(src("ckr", 16, 32), (8, 1)),
             src("ak", 0, 64), src("ak", 0, 64), src("ak", 64, 128), src("ak", 64, 128),
             src("av", 0, 64), src("av", 0, 64), src("av", 64, 128), src("av", 64, 128),
             src("ik"), src("ik"),
             src("iw"), zeros(124)]
    out = jnp.concatenate(parts, axis=0)
    assert out.shape[0] == N_PROJ
    return out.astype(BF16)


def _rms(x, g):
    inv = lax.rsqrt(jnp.mean(x * x, axis=-1, keepdims=True) + EPS)
    return (x * inv) * g


def _dot(a, b):
    return jnp.dot(a, b, preferred_element_type=F32)


def _dot_nt(a, b):
    return lax.dot_general(a, b, (((1,), (1,)), ((), ())), preferred_element_type=F32)


def _inproj_kernel(x_ref, g_ref, w_ref, o_ref, h_ref):
    @pl.when(pl.program_id(1) == 0)
    def _():
        h_ref[...] = _rms(x_ref[...], g_ref[...]).astype(BF16)

    o_ref[...] = _dot_nt(h_ref[...], w_ref[...])


def _inproj(x, g, w, tm):
    m = x.shape[0]
    tm = min(tm, m)
    return pl.pallas_call(
        _inproj_kernel,
        grid=(m // tm, N_PROJ // PROJ_TN),
        in_specs=[pl.BlockSpec((tm, D_MODEL), lambda i, j: (i, 0)),
                  pl.BlockSpec((1, D_MODEL), lambda i, j: (0, 0)),
                  pl.BlockSpec((PROJ_TN, D_MODEL), lambda i, j: (j, 0))],
        out_specs=pl.BlockSpec((tm, PROJ_TN), lambda i, j: (i, j)),
        out_shape=jax.ShapeDtypeStruct((m, N_PROJ), F32),
        scratch_shapes=[pltpu.VMEM((tm, D_MODEL), BF16)],
        compiler_params=_cparams(("parallel", "arbitrary")),
        name="in_proj",
    )(x, g, w)


def _cprep_kernel(cq_ref, ckv_ref, ckr_ref, cos_ref, sin_ref, gq_ref, gkv_ref, wuq_ref, wuk_ref,
                  q_ref, kcat_ref, ckvn_ref, kr_ref, qr_ref):
    cos = cos_ref[...]
    sin = sin_ref[...]
    cqn = _rms(cq_ref[...], gq_ref[...]).astype(BF16)
    q = _dot(cqn, wuq_ref[...])
    nope = C_HEADS * C_NOPE
    r1 = q[:, nope:nope + LANES]
    r2 = q[:, nope + LANES:nope + 2 * LANES]
    o1 = r1 * cos - r2 * sin
    o2 = r1 * sin + r2 * cos
    qr_ref[...] = jnp.concatenate([o1, o2], axis=-1)
    head_of_lane = lax.broadcasted_iota(jnp.int32, o1.shape, 1) // (C_ROPE // 2)
    for p in range(C_HEADS // 2):
        ql = _dot(q[:, LANES * p:LANES * (p + 1)].astype(BF16), wuk_ref[p])
        for e in range(2):
            h = 2 * p + e
            sel = head_of_lane == h
            q_ref[h] = jnp.concatenate(
                [ql[:, C_KV_RANK * e:C_KV_RANK * (e + 1)],
                 jnp.where(sel, o1, 0.0), jnp.where(sel, o2, 0.0)], axis=-1).astype(BF16)
    ckvn = _rms(ckv_ref[...], gkv_ref[...])
    ckvn_ref[...] = ckvn
    c = ckr_ref[...]
    c1 = c[:, :LANES]
    c2 = c[:, LANES:]
    k1 = c1 * cos - c2 * sin
    k2 = c1 * sin + c2 * cos
    kr_ref[...] = jnp.concatenate([k1, k2], axis=-1)
    kcat_ref[...] = jnp.concatenate([ckvn, k1, k2], axis=-1).astype(BF16)


def _cprep(z, cos, sin, tab_map, g_cq, g_ckv, wuq, wukp, tm):
    m = z.shape[0]
    tm = min(tm, m)
    qw = C_KV_RANK + 2 * LANES
    const2 = lambda i: (0, 0)
    return pl.pallas_call(
        _cprep_kernel,
        grid=(m // tm,),
        in_specs=[pl.BlockSpec((tm, 256), lambda i: (i, _blk("cq", 256))),
                  pl.BlockSpec((tm, 256), lambda i: (i, _blk("ckv", 256))),
                  pl.BlockSpec((tm, 256), lambda i: (i, _blk("ckr", 256))),
                  pl.BlockSpec((tm, LANES), tab_map),
                  pl.BlockSpec((tm, LANES), tab_map),
                  pl.BlockSpec((1, C_Q_RANK), const2),
                  pl.BlockSpec((1, C_KV_RANK), const2),
                  pl.BlockSpec(wuq.shape, const2),
                  pl.BlockSpec(wukp.shape, lambda i: (0, 0, 0))],
        out_specs=[pl.BlockSpec((C_HEADS, tm, qw), lambda i: (0, i, 0)),
                   pl.BlockSpec((tm, qw), lambda i: (i, 0)),
                   pl.BlockSpec((tm, C_KV_RANK), lambda i: (i, 0)),
                   pl.BlockSpec((tm, 2 * LANES), lambda i: (i, 0)),
                   pl.BlockSpec((tm, 2 * LANES), lambda i: (i, 0))],
        out_shape=[jax.ShapeDtypeStruct((C_HEADS, m, qw), BF16),
                   jax.ShapeDtypeStruct((m, qw), BF16),
                   jax.ShapeDtypeStruct((m, C_KV_RANK), F32),
                   jax.ShapeDtypeStruct((m, 2 * LANES), F32),
                   jax.ShapeDtypeStruct((m, 2 * LANES), F32)],
        compiler_params=_cparams(("parallel",)),
        name="c_prep",
    )(z, z, z, cos, sin, g_cq, g_ckv, wuq, wukp)


MLA_CHUNK_ROWS = 256
MLA_KEY_BLOCK = 512
MLA_QUERY_BLOCK = 256


def _mla_kernel(q_ref, k_ref, wuv_ref, o_ref, m_ref, l_ref, acc_ref, *, tk, nkb, qb):
    i = pl.program_id(1)
    j = pl.program_id(2)
    chunk_heads = max(1, MLA_CHUNK_ROWS // qb)
    chunk_rows = chunk_heads * qb

    @pl.when(j == 0)
    def _():
        m_ref[...] = jnp.full(m_ref.shape, NEG_BIG, F32)
        l_ref[...] = jnp.zeros(l_ref.shape, F32)
        acc_ref[...] = jnp.zeros(acc_ref.shape, F32)

    def step(masked):
        k = k_ref[...]
        v = k[:, :C_KV_RANK]
        n_chunk = C_HEADS // chunk_heads

        def scores(c):
            q = q_ref[c * chunk_heads:(c + 1) * chunk_heads].reshape(chunk_rows, q_ref.shape[-1])
            return _dot_nt(q, k) * C_SCALE

        lane_tiles = lambda x: [x[:, LANES * u:LANES * (u + 1)] for u in range(x.shape[1] // LANES)]
        s_next = scores(0)
        for c in range(n_chunk):
            r0, r1 = c * chunk_rows, (c + 1) * chunk_rows
            s = s_next
            if c + 1 < n_chunk:
                s_next = scores(c + 1)
            if masked:
                t = i * qb + lax.broadcasted_iota(jnp.int32, s.shape, 0) % qb
                col = j * tk + lax.broadcasted_iota(jnp.int32, s.shape, 1)
                s = jnp.where(col <= t, s, -jnp.inf)
            m_old = m_ref[r0:r1, :]
            m_new = jnp.maximum(m_old, jnp.max(s, axis=1, keepdims=True))
            alpha = jnp.exp(m_old - m_new)
            p_tiles = [jnp.exp(st - m_new) for st in lane_tiles(s)]
            l_ref[r0:r1, :] = alpha * l_ref[r0:r1, :] + sum(p_tiles)
            pv = _dot(jnp.concatenate(p_tiles, axis=-1).astype(BF16), v)
            alpha_wide = jnp.concatenate([alpha] * (C_KV_RANK // LANES), axis=-1)
            acc_ref[r0:r1, :] = alpha_wide * acc_ref[r0:r1, :] + pv
            m_ref[r0:r1, :] = m_new

    first_row = i * qb
    fully_visible = j * tk + tk - 1 <= first_row
    pl.when(fully_visible)(functools.partial(step, False))
    pl.when(jnp.logical_and(jnp.logical_not(fully_visible), j * tk <= first_row + qb - 1))(
        functools.partial(step, True))

    @pl.when(j == nkb - 1)
    def _():
        o = acc_ref[...] / jnp.sum(l_ref[...], axis=1, keepdims=True)
        outs = []
        for p in range(C_HEADS // 2):
            pair = jnp.concatenate([o[2 * p * qb:(2 * p + 1) * qb],
                                    o[(2 * p + 1) * qb:(2 * p + 2) * qb]], axis=-1)
            outs.append(_dot(pair.astype(BF16), wuv_ref[p]))
        o_ref[...] = jnp.concatenate(outs, axis=-1).astype(o_ref.dtype)


def _mla_prompt(q, kcat, wuvp, bn, s_len):
    qb = min(MLA_QUERY_BLOCK, s_len)
    nq = s_len // qb
    tk = min(MLA_KEY_BLOCK, s_len)
    nkb = s_len // tk
    qw = q.shape[-1]

    def k_map(b, i, j):
        last = (i * qb + qb - 1) // tk
        return (b * nkb + jnp.minimum(j, last), 0)

    return pl.pallas_call(
        functools.partial(_mla_kernel, tk=tk, nkb=nkb, qb=qb),
        grid=(bn, nq, nkb),
        in_specs=[pl.BlockSpec((C_HEADS, qb, qw), lambda b, i, j: (0, b * nq + i, 0)),
                  pl.BlockSpec((tk, qw), k_map),
                  pl.BlockSpec(wuvp.shape, lambda b, i, j: (0, 0, 0))],
        out_specs=pl.BlockSpec((qb, C_WIDTH), lambda b, i, j: (b * nq + i, 0)),
        out_shape=jax.ShapeDtypeStruct((bn * s_len, C_WIDTH), BF16),
        scratch_shapes=[pltpu.VMEM((C_HEADS * qb, LANES), F32),
                        pltpu.VMEM((C_HEADS * qb, LANES), F32),
                        pltpu.VMEM((C_HEADS * qb, C_KV_RANK), F32)],
        compiler_params=_cparams(("parallel", "parallel", "arbitrary")),
        name="mla_prompt",
    )(q, kcat, wuvp)


def _rowsum(x):
    return jnp.sum(x, axis=1, keepdims=True)


def _topk_select(xs, allowed, col, k, nbits, n_bisect):
    kf = float(k)
    inf = jnp.float32(jnp.inf)
    row_min = jnp.min(jnp.where(allowed, xs, inf), axis=1, keepdims=True)
    row_max = jnp.max(xs, axis=1, keepdims=True)
    n_allowed = _rowsum(jnp.where(allowed, 1.0, 0.0))

    def bisect(_, c):
        low, high, has_low = c
        mid = low + (high - low) * 0.5
        up = _rowsum(jnp.where(xs > mid, 1.0, 0.0)) >= kf
        return jnp.where(up, mid, low), jnp.where(up, high, mid), jnp.where(up, 1.0, has_low)

    low, high, has_low = lax.fori_loop(0, n_bisect, bisect, (row_min, row_max, jnp.zeros_like(row_min)))
    lo0 = jnp.min(jnp.where(xs > low, xs, inf), axis=1, keepdims=True)
    lo0 = jnp.where((has_low > 0.5) & (n_allowed > kf), lo0, row_min)
    ub0 = jnp.max(jnp.where(xs > high, -inf, xs), axis=1, keepdims=True)
    ub0 = jnp.where(n_allowed > kf, ub0, row_min)

    def cond(c):
        lo, ub = c
        return jnp.max(jnp.where(lo < ub, 1.0, 0.0)) > 0.0

    def body(c):
        lo, ub = c
        mid = lo + (ub - lo) * 0.5
        mid = jnp.where((mid >= lo) & (mid < ub), mid, lo)
        gt = xs > mid
        cnt = _rowsum(jnp.where(gt, 1.0, 0.0))
        vmin = jnp.min(jnp.where(gt, xs, inf), axis=1, keepdims=True)
        vmax = jnp.max(jnp.where(gt, -inf, xs), axis=1, keepdims=True)
        active = lo < ub
        up = cnt >= kf
        lo = jnp.where(active & up, vmin, lo)
        ub = jnp.where(active & jnp.logical_not(up), vmax, ub)
        return lo, ub

    v, _ = lax.while_loop(cond, body, (lo0, ub0))
    gtv = xs > v
    need = kf - _rowsum(jnp.where(gtv, 1.0, 0.0))
    tie = allowed & (xs == v)
    n_tie = _rowsum(jnp.where(tie, 1.0, 0.0))

    def search(_):
        def step(it, c):
            cand = c + jnp.left_shift(jnp.int32(1), nbits - 1 - it)
            f = _rowsum(jnp.where(tie & (col < cand), 1.0, 0.0))
            return jnp.where(f < need, cand, c)
        return lax.fori_loop(0, nbits, step, jnp.zeros(v.shape, jnp.int32))

    def no_search(_):
        return jnp.full(v.shape, (1 << nbits) - 1, jnp.int32)

    any_over = jnp.max(jnp.where(n_tie > need, 1.0, 0.0)) > 0.0
    cut = lax.cond(any_over, search, no_search, 0)
    return gtv | (tie & (col <= cut))


KEY_CLASS = 256
N_BISECT_PROMPT = 14
N_BISECT_SAMPLE = 16


def _sparse_prompt_kernel(q_ref, iq_ref, iw_ref, kd_ref, vd_ref, ikd_ref, o_ref, kb_ref, vb_ref, ib_ref,
                          *, topk, s_len):
    i = pl.program_id(1)

    @pl.when(i == 0)
    def _():
        ib_ref[...] = ikd_ref[...].astype(BF16)
        for g in range(A_KV_HEADS):
            kb_ref[g] = kd_ref[:, LANES * g:LANES * (g + 1)].astype(BF16)
            vb_ref[g] = vd_ref[:, LANES * g:LANES * (g + 1)].astype(BF16)

    first_half = lax.broadcasted_iota(jnp.int32, (Q_BLOCK, LANES), 1) < (LANES // 2)

    def half(x, pair, e):
        blk = x[:, LANES * pair:LANES * (pair + 1)]
        return jnp.where(first_half if e == 0 else jnp.logical_not(first_half), blk, 0.0).astype(BF16)

    def body(width):
        q = q_ref[...] * A_SCALE
        iq = iq_ref[...]
        iw = iw_ref[...]
        ikd = ib_ref[0:width, :]
        score = None
        for h in range(IDX_HEADS):
            rel = jnp.maximum(_dot_nt(half(iq, h // 2, h % 2), ikd), 0.0)
            term = iw[:, h:h + 1] * rel
            score = term if score is None else score + term
        t = i * Q_BLOCK + lax.broadcasted_iota(jnp.int32, score.shape, 0)
        col = lax.broadcasted_iota(jnp.int32, score.shape, 1)
        allowed = col <= t
        xs = jnp.where(allowed, score + 0.0, -jnp.inf)
        nbits = max(1, (width - 1).bit_length())
        sel = _topk_select(xs, allowed, col, topk, nbits, N_BISECT_PROMPT)
        bias = jnp.where(sel, 0.0, -jnp.inf)

        rep = A_HEADS // A_KV_HEADS
        outs = []
        for g in range(A_KV_HEADS):
            kd = kb_ref[g, 0:width, :]
            vd = vb_ref[g, 0:width, :]
            heads = [g * rep + r for r in range(rep)]
            s_all = _dot_nt(jnp.concatenate([half(q, h // 2, h % 2) for h in heads], axis=0), kd)
            ps, ls = [], []
            for r in range(rep):
                s = s_all[r * Q_BLOCK:(r + 1) * Q_BLOCK] + bias
                m = jnp.max(s, axis=1, keepdims=True)
                p = jnp.exp(s - m)
                ls.append(_rowsum(p))
                ps.append(p.astype(BF16))
            o_all = _dot(jnp.concatenate(ps, axis=0), vd)
            for r in range(rep):
                outs.append(o_all[r * Q_BLOCK:(r + 1) * Q_BLOCK] / ls[r])
        pairs = [jnp.where(first_half, outs[2 * j], outs[2 * j + 1]) for j in range(A_HEADS // 2)]
        o_ref[...] = jnp.concatenate(pairs, axis=-1).astype(o_ref.dtype)

    per_class = KEY_CLASS // Q_BLOCK
    n_class = max(1, s_len // KEY_CLASS)
    for c in range(n_class):
        width = min(s_len, (c + 1) * KEY_CLASS)
        pl.when(i // per_class == c)(functools.partial(body, width))


def _sparse_prompt(z3, topk):
    bn, s_len, _ = z3.shape
    nq = s_len // Q_BLOCK
    return pl.pallas_call(
        functools.partial(_sparse_prompt_kernel, topk=topk, s_len=s_len),
        grid=(bn, nq),
        in_specs=[pl.BlockSpec((None, Q_BLOCK, 512), lambda b, i: (b, i, _blk("aq", 512))),
                  pl.BlockSpec((None, Q_BLOCK, 256), lambda b, i: (b, i, _blk("iq", 256))),
                  pl.BlockSpec((None, Q_BLOCK, 128), lambda b, i: (b, i, _blk("iw", 128))),
                  pl.BlockSpec((None, s_len, 256), lambda b, i: (b, 0, _blk("ak", 256))),
                  pl.BlockSpec((None, s_len, 256), lambda b, i: (b, 0, _blk("av", 256))),
                  pl.BlockSpec((None, s_len, 128), lambda b, i: (b, 0, _blk("ik", 128)))],
        out_specs=pl.BlockSpec((None, Q_BLOCK, A_WIDTH), lambda b, i: (b, i, 0)),
        out_shape=jax.ShapeDtypeStruct((bn, s_len, A_WIDTH), BF16),
        scratch_shapes=[pltpu.VMEM((A_KV_HEADS, s_len, LANES), BF16),
                        pltpu.VMEM((A_KV_HEADS, s_len, LANES), BF16),
                        pltpu.VMEM((s_len, LANES), BF16)],
        compiler_params=_cparams(("parallel", "arbitrary")),
        name="sparse_prompt",
    )(z3, z3, z3, z3, z3, z3)


PAD_ROWS = 8


def _bconv_kernel(gb_ref, gc_ref, xb_ref, w_ref, y_ref, tail_ref, pad_ref):
    tb = gb_ref.shape[0]

    @pl.when(pl.program_id(1) == 0)
    def _():
        pad_ref[0:PAD_ROWS, :] = jnp.zeros((PAD_ROWS, B_WIDTH), F32)

    p = gc_ref[...] * xb_ref[...]
    pad_ref[PAD_ROWS:PAD_ROWS + tb, :] = p
    w = w_ref[...]
    y = (pad_ref[PAD_ROWS - 2:PAD_ROWS - 2 + tb, :] * w[0:1]
         + pad_ref[PAD_ROWS - 1:PAD_ROWS - 1 + tb, :] * w[1:2]
         + p * w[2:3])
    y_ref[...] = (gb_ref[...] * y).astype(y_ref.dtype)
    tail_ref[...] = pad_ref[PAD_ROWS + tb - (B_CONV - 1):PAD_ROWS + tb, :]
    pad_ref[0:PAD_ROWS, :] = pad_ref[tb:tb + PAD_ROWS, :]


def _bconv_prompt(z3, conv_w):
    bn, s_len, _ = z3.shape
    tb = min(256, s_len)
    nt = s_len // tb
    spec = lambda name: pl.BlockSpec((None, tb, 512), lambda b, t: (b, t, _blk(name, 512)))
    return pl.pallas_call(
        _bconv_kernel,
        grid=(bn, nt),
        in_specs=[spec("gb"), spec("gc"), spec("xb"),
                  pl.BlockSpec((B_CONV, B_WIDTH), lambda b, t: (0, 0))],
        out_specs=[pl.BlockSpec((None, tb, B_WIDTH), lambda b, t: (b, t, 0)),
                   pl.BlockSpec((None, B_CONV - 1, B_WIDTH), lambda b, t: (b, 0, 0))],
        out_shape=[jax.ShapeDtypeStruct((bn, s_len, B_WIDTH), BF16),
                   jax.ShapeDtypeStruct((bn, B_CONV - 1, B_WIDTH), F32)],
        scratch_shapes=[pltpu.VMEM((PAD_ROWS + tb, B_WIDTH), F32)],
        compiler_params=_cparams(("parallel", "arbitrary")),
        name="bconv_prompt",
    )(z3, z3, z3, conv_w)


def _lru_coeffs(xc, wr, br, wi, bi, lam):
    xcb = xc.astype(BF16)
    r = jax.nn.sigmoid(_dot(xcb, wr) + br)
    ig = jax.nn.sigmoid(_dot(xcb, wi) + bi)
    nl = -lam
    softplus = jnp.maximum(nl, 0.0) + jnp.log1p(jnp.exp(-jnp.abs(nl)))
    log_a = -LRU_C * r * softplus
    a = jnp.exp(log_a)
    th = jnp.tanh(log_a)
    u = jnp.sqrt(-2.0 * th / (1.0 - th)) * (ig * xc)
    return a, u


def _rglru_kernel(dx_ref, dg_ref, cw_ref, cb_ref, wr_ref, br_ref, wi_ref, bi_ref, lam_ref,
                  y_ref, dtail_ref, hlast_ref, pad_ref, h_ref):
    tb = dx_ref.shape[0]

    @pl.when(pl.program_id(1) == 0)
    def _():
        pad_ref[0:PAD_ROWS, :] = jnp.zeros((PAD_ROWS, D_WIDTH), F32)
        h_ref[...] = jnp.zeros(h_ref.shape, F32)

    x = dx_ref[...]
    pad_ref[PAD_ROWS:PAD_ROWS + tb, :] = x
    cw = cw_ref[...]
    xc = pad_ref[PAD_ROWS - 3:PAD_ROWS - 3 + tb, :] * cw[0:1]
    xc = xc + pad_ref[PAD_ROWS - 2:PAD_ROWS - 2 + tb, :] * cw[1:2]
    xc = xc + pad_ref[PAD_ROWS - 1:PAD_ROWS - 1 + tb, :] * cw[2:3]
    xc = xc + x * cw[3:4]
    xc = xc + cb_ref[...]
    a, u = _lru_coeffs(xc, wr_ref[...], br_ref[...], wi_ref[...], bi_ref[...], lam_ref[...])
    row = lax.broadcasted_iota(jnp.int32, a.shape, 0)
    d = 1
    while d < tb:
        keep = row >= d
        a_prev = jnp.where(keep, pltpu.roll(a, d, 0), 1.0)
        u_prev = jnp.where(keep, pltpu.roll(u, d, 0), 0.0)
        u = a * u_prev + u
        a = a * a_prev
        d *= 2
    h = a * h_ref[0:1, :] + u
    y_ref[...] = (jax.nn.gelu(dg_ref[...]) * h).astype(y_ref.dtype)
    h_ref[0:1, :] = h[tb - 1:tb, :]
    hlast_ref[...] = h[tb - 1:tb, :]
    dtail_ref[...] = pad_ref[PAD_ROWS + tb - (D_CONV - 1):PAD_ROWS + tb, :]
    pad_ref[0:PAD_ROWS, :] = pad_ref[tb:tb + PAD_ROWS, :]


def _rglru_prompt(z3, cw, cb, wr, br, wi, bi, lam):
    bn, s_len, _ = z3.shape
    tb = min(256, s_len)
    nt = s_len // tb
    spec = lambda name: pl.BlockSpec((None, tb, 512), lambda b, t: (b, t, _blk(name, 512)))
    vec = pl.BlockSpec((1, D_WIDTH), lambda b, t: (0, 0))
    mat = pl.BlockSpec((D_WIDTH, D_WIDTH), lambda b, t: (0, 0))
    return pl.pallas_call(
        _rglru_kernel,
        grid=(bn, nt),
        in_specs=[spec("dx"), spec("dg"), pl.BlockSpec((D_CONV, D_WIDTH), lambda b, t: (0, 0)),
                  vec, mat, vec, mat, vec, vec],
        out_specs=[pl.BlockSpec((None, tb, D_WIDTH), lambda b, t: (b, t, 0)),
                   pl.BlockSpec((None, D_CONV - 1, D_WIDTH), lambda b, t: (b, 0, 0)),
                   pl.BlockSpec((None, 1, D_WIDTH), lambda b, t: (b, 0, 0))],
        out_shape=[jax.ShapeDtypeStruct((bn, s_len, D_WIDTH), BF16),
                   jax.ShapeDtypeStruct((bn, D_CONV - 1, D_WIDTH), F32),
                   jax.ShapeDtypeStruct((bn, 1, D_WIDTH), F32)],
        scratch_shapes=[pltpu.VMEM((PAD_ROWS + tb, D_WIDTH), F32),
                        pltpu.VMEM((8, D_WIDTH), F32)],
        compiler_params=_cparams(("parallel", "arbitrary")),
        name="rglru_prompt",
    )(z3, z3, cw, cb, wr, br, wi, bi, lam)


def _sample_bd_kernel(gb_ref, gc_ref, xb_ref, dg_ref, dx_ref, sb_ref, sd_ref, h0_ref,
                      bw_ref, cw_ref, cb_ref, wr_ref, br_ref, wi_ref, bi_ref, lam_ref,
                      yb_ref, yd_ref, bnew_ref, dnew_ref, hnew_ref):
    w = B_WIDTH
    p = gc_ref[...] * xb_ref[...]
    bw = bw_ref[...]
    b0 = sb_ref[:, 0:w]
    b1 = sb_ref[:, w:2 * w]
    y = b0 * bw[0:1] + b1 * bw[1:2] + p * bw[2:3]
    yb_ref[...] = (gb_ref[...] * y).astype(yb_ref.dtype)
    bnew_ref[:, 0:w] = b1
    bnew_ref[:, w:2 * w] = p

    w = D_WIDTH
    x = dx_ref[...]
    cw = cw_ref[...]
    d0 = sd_ref[:, 0:w]
    d1 = sd_ref[:, w:2 * w]
    d2 = sd_ref[:, 2 * w:3 * w]
    xc = d0 * cw[0:1]
    xc = xc + d1 * cw[1:2]
    xc = xc + d2 * cw[2:3]
    xc = xc + x * cw[3:4]
    xc = xc + cb_ref[...]
    a, u = _lru_coeffs(xc, wr_ref[...], br_ref[...], wi_ref[...], bi_ref[...], lam_ref[...])
    h = a * h0_ref[...] + u
    hnew_ref[...] = h
    yd_ref[...] = (jax.nn.gelu(dg_ref[...]) * h).astype(yd_ref.dtype)
    dnew_ref[:, 0:w] = d1
    dnew_ref[:, w:2 * w] = d2
    dnew_ref[:, 2 * w:3 * w] = x


def _sample_bd(z, sb, sd, h0, bw, cw, cb, wr, br, wi, bi, lam):
    m = z.shape[0]
    zspec = lambda name: pl.BlockSpec((m, 512), lambda i: (0, _blk(name, 512)))
    full = lambda a: pl.BlockSpec(a.shape, lambda i: (0,) * a.ndim)
    args = (sb, sd, h0, bw, cw, cb, wr, br, wi, bi, lam)
    out_shapes = [jax.ShapeDtypeStruct((m, B_WIDTH), BF16),
                  jax.ShapeDtypeStruct((m, D_WIDTH), BF16),
                  jax.ShapeDtypeStruct((m, (B_CONV - 1) * B_WIDTH), F32),
                  jax.ShapeDtypeStruct((m, (D_CONV - 1) * D_WIDTH), F32),
                  jax.ShapeDtypeStruct((m, D_WIDTH), F32)]
    return pl.pallas_call(
        _sample_bd_kernel,
        grid=(1,),
        in_specs=[zspec("gb"), zspec("gc"), zspec("xb"), zspec("dg"), zspec("dx")]
                 + [full(a) for a in args],
        out_specs=[pl.BlockSpec(s.shape, lambda i: (0, 0)) for s in out_shapes],
        out_shape=out_shapes,
        compiler_params=_cparams(("arbitrary",)),
        name="sample_bd",
    )(z, z, z, z, z, *args)


def _merge_kernel(ya_ref, yb_ref, yc_ref, yd_ref, g0_ref, g1_ref, g2_ref, g3_ref, x_ref,
                  wa_ref, wb_ref, wc_ref, wd_ref, wo_ref, gp_ref, o_ref):
    acc = jax.nn.sigmoid(g0_ref[...]) * _dot(ya_ref[...], wa_ref[...])
    acc = acc + jax.nn.sigmoid(g1_ref[...]) * _dot(yb_ref[...], wb_ref[...])
    acc = acc + jax.nn.sigmoid(g2_ref[...]) * _dot(yc_ref[...], wc_ref[...])
    acc = acc + jax.nn.sigmoid(g3_ref[...]) * _dot(yd_ref[...], wd_ref[...])
    m = _dot(acc.astype(BF16), wo_ref[...])
    o_ref[...] = x_ref[...] + _rms(m, gp_ref[...])


def _merge(ya, yb, yc, yd, z, x, wa, wb, wc, wd, wo, gp, tm):
    m = x.shape[0]
    tm = min(tm, m)
    row = lambda w: pl.BlockSpec((tm, w), lambda i: (i, 0))
    gate = lambda k: pl.BlockSpec((tm, D_MODEL), lambda i: (i, _blk("gt", D_MODEL) + k))
    const = lambda a: pl.BlockSpec(a.shape, lambda i: (0, 0))
    return pl.pallas_call(
        _merge_kernel,
        grid=(m // tm,),
        in_specs=[row(512), row(512), row(512), row(512), gate(0), gate(1), gate(2), gate(3),
                  row(D_MODEL), const(wa), const(wb), const(wc), const(wd), const(wo), const(gp)],
        out_specs=row(D_MODEL),
        out_shape=jax.ShapeDtypeStruct((m, D_MODEL), F32),
        compiler_params=_cparams(("parallel",)),
        name="merge",
    )(ya, yb, yc, yd, z, z, z, z, x, wa, wb, wc, wd, wo, gp)


def _ffn_kernel(x_ref, gpre_ref, wg_ref, wu_ref, wd_ref, gpost_ref, o_ref, h_ref, acc_ref, *, nf):
    j = pl.program_id(1)

    @pl.when(j == 0)
    def _():
        h_ref[...] = _rms(x_ref[...], gpre_ref[...]).astype(BF16)
        acc_ref[...] = jnp.zeros(acc_ref.shape, F32)

    h = h_ref[...]
    act = jax.nn.silu(_dot(h, wg_ref[...])) * _dot(h, wu_ref[...])
    acc_ref[...] += _dot(act.astype(BF16), wd_ref[...])

    @pl.when(j == nf - 1)
    def _():
        o_ref[...] = x_ref[...] + _rms(acc_ref[...], gpost_ref[...])


def _ffn(x, gpre, wg, wu, wd, gpost, tm, tf=256):
    m = x.shape[0]
    tm = min(tm, m)
    nf = D_FF // tf
    return pl.pallas_call(
        functools.partial(_ffn_kernel, nf=nf),
        grid=(m // tm, nf),
        in_specs=[pl.BlockSpec((tm, D_MODEL), lambda i, j: (i, 0)),
                  pl.BlockSpec((1, D_MODEL), lambda i, j: (0, 0)),
                  pl.BlockSpec((D_MODEL, tf), lambda i, j: (0, j)),
                  pl.BlockSpec((D_MODEL, tf), lambda i, j: (0, j)),
                  pl.BlockSpec((tf, D_MODEL), lambda i, j: (j, 0)),
                  pl.BlockSpec((1, D_MODEL), lambda i, j: (0, 0))],
        out_specs=pl.BlockSpec((tm, D_MODEL), lambda i, j: (i, 0)),
        out_shape=jax.ShapeDtypeStruct((m, D_MODEL), F32),
        scratch_shapes=[pltpu.VMEM((tm, D_MODEL), BF16), pltpu.VMEM((tm, D_MODEL), F32)],
        compiler_params=_cparams(("parallel", "arbitrary")),
        name="ffn",
    )(x, gpre, wg, wu, wd, gpost)


IDX_ROWS = 8
PAGES_PER_SCORE_STEP = 32
PAGES_PER_ATTN_STEP = 32


def _sample_scores_kernel(pt_ref, iq_ref, iw_ref, ikn_ref, *rest, g_pages, n_pages):
    ki_refs = rest[:g_pages]
    o_ref = rest[g_pages]
    j = pl.program_id(1)
    qi = iq_ref[...].astype(BF16)
    wi = iw_ref[...]
    rel = jnp.concatenate([_dot(qi, ki_refs[g][...].astype(BF16)) for g in range(g_pages)], axis=-1)
    sc = jnp.sum(wi * jnp.maximum(rel, 0.0), axis=0, keepdims=True)
    start = pl.multiple_of(j * (g_pages * PAGE_SIZE), g_pages * PAGE_SIZE)
    o_ref[:, pl.ds(start, g_pages * PAGE_SIZE)] = sc

    @pl.when(j == n_pages // g_pages - 1)
    def _():
        kn = ikn_ref[...].astype(BF16).astype(F32)
        rel = jnp.maximum(jnp.sum(qi.astype(F32) * kn, axis=1, keepdims=True), 0.0)
        sc = jnp.sum(wi * rel, axis=0, keepdims=True)
        lane = lax.broadcasted_iota(jnp.int32, (1, PAGE_SIZE), 1)
        o_ref[:, n_pages * PAGE_SIZE:(n_pages + 1) * PAGE_SIZE] = jnp.where(lane == 0, sc, -jnp.inf)


def _sample_scores(pt_flat, iq3, iw3, ikn3, cache_ki, layer, n_pages, g_pages):
    bd = iq3.shape[0]
    ns = (n_pages + 1) * PAGE_SIZE

    def page_map(g):
        return lambda b, j, pt: (layer, pt[b * n_pages + j * g_pages + g], 0, 0)

    grid_spec = pltpu.PrefetchScalarGridSpec(
        num_scalar_prefetch=1,
        grid=(bd, n_pages // g_pages),
        in_specs=[pl.BlockSpec((None, IDX_ROWS, IDX_DIM), lambda b, j, pt: (b, 0, 0)),
                  pl.BlockSpec((None, IDX_ROWS, 1), lambda b, j, pt: (b, 0, 0)),
                  pl.BlockSpec((None, 1, IDX_DIM), lambda b, j, pt: (b, 0, 0))]
                 + [pl.BlockSpec((None, None, IDX_DIM, PAGE_SIZE), page_map(g)) for g in range(g_pages)],
        out_specs=pl.BlockSpec((None, 1, ns), lambda b, j, pt: (b, 0, 0)),
    )
    return pl.pallas_call(
        functools.partial(_sample_scores_kernel, g_pages=g_pages, n_pages=n_pages),
        grid_spec=grid_spec,
        out_shape=jax.ShapeDtypeStruct((bd, 1, ns), F32),
        compiler_params=_cparams(("parallel", "arbitrary")),
        name="sample_scores",
    )(pt_flat, iq3, iw3, ikn3, *([cache_ki] * g_pages))


def _sample_select_kernel(s_ref, o_ref, *, topk, nbits, n_valid):
    xs0 = s_ref[...]
    col = lax.broadcasted_iota(jnp.int32, xs0.shape, 1)
    allowed = col < n_valid
    xs = jnp.where(allowed, xs0 + 0.0, -jnp.inf)
    sel = _topk_select(xs, allowed, col, topk, nbits, N_BISECT_SAMPLE)
    o_ref[...] = jnp.where(sel, 1.0, 0.0)


def _sample_select(scores, topk, n_valid):
    bd, ns = scores.shape
    nbits = max(1, (ns - 1).bit_length())
    return pl.pallas_call(
        functools.partial(_sample_select_kernel, topk=topk, nbits=nbits, n_valid=n_valid),
        grid=(1,),
        in_specs=[pl.BlockSpec((bd, ns), lambda i: (0, 0))],
        out_specs=pl.BlockSpec((bd, ns), lambda i: (0, 0)),
        out_shape=jax.ShapeDtypeStruct((bd, ns), F32),
        compiler_params=_cparams(("arbitrary",)),
        name="sample_select",
    )(scores)


def _online_update(s, weighted_values, m_ref, l_ref, acc_ref):
    m_old = m_ref[...]
    m_new = jnp.maximum(m_old, jnp.max(s, axis=1, keepdims=True))
    alpha = jnp.exp(m_old - m_new)
    p = jnp.exp(s - m_new)
    l_ref[...] = alpha * l_ref[...] + _rowsum(p)
    acc_ref[...] = alpha * acc_ref[...] + weighted_values(p.astype(BF16))
    m_ref[...] = m_new


def _online_update_one(s, v_row, m_ref, l_ref, acc_ref):
    m_old = m_ref[...]
    m_new = jnp.maximum(m_old, s)
    alpha = jnp.exp(m_old - m_new)
    p = jnp.exp(s - m_new)
    l_ref[...] = alpha * l_ref[...] + p
    acc_ref[...] = alpha * acc_ref[...] + p.astype(BF16).astype(F32) * v_row
    m_ref[...] = m_new


def _sample_attn_kernel(pt_ref, mask_ref, qa_ref, ql_ref, qr_ref, kn_ref, vn_ref, cn_ref, rn_ref,
                        k_hbm, v_hbm, c_hbm, r_hbm, oa_ref, oc_ref, kbuf, vbuf, cbuf, rbuf, sems,
                        ma_ref, la_ref, acca_ref, mc_ref, lc_ref, accc_ref, *, layer, g_pages, n_pages):
    b = pl.program_id(0)
    n_groups = n_pages // g_pages
    streams = ((k_hbm, kbuf), (v_hbm, vbuf), (c_hbm, cbuf), (r_hbm, rbuf))

    def slot_of(sample, jg):
        return jg % 2 if n_groups % 2 == 0 else (sample * n_groups + jg) % 2

    def group_copies(sample, jg):
        slot = slot_of(sample, jg)
        copies = []
        for g in range(g_pages):
            page_id = pt_ref[sample * n_pages + jg * g_pages + g]
            for a, (hbm, buf) in enumerate(streams):
                copies.append(pltpu.make_async_copy(hbm.at[layer, page_id], buf.at[slot, g], sems.at[slot, a]))
        return copies

    def start(sample, jg):
        for c in group_copies(sample, jg):
            c.start()

    def wait(sample, jg):
        for c in group_copies(sample, jg):
            c.wait()

    @pl.when(b == 0)
    def _():
        start(b, 0)

    for m_ref, l_ref, acc_ref in ((ma_ref, la_ref, acca_ref), (mc_ref, lc_ref, accc_ref)):
        m_ref[...] = jnp.full(m_ref.shape, NEG_BIG, F32)
        l_ref[...] = jnp.zeros(l_ref.shape, F32)
        acc_ref[...] = jnp.zeros(acc_ref.shape, F32)

    qa = qa_ref[...].astype(BF16)
    ql = ql_ref[...]
    qr = qr_ref[...].astype(BF16)
    span = g_pages * PAGE_SIZE
    page = lambda x, g: x[:, g * PAGE_SIZE:(g + 1) * PAGE_SIZE]

    for jg in range(n_groups):
        if jg + 1 < n_groups:
            start(b, jg + 1)
        else:
            pl.when(b + 1 < pl.num_programs(0))(functools.partial(start, b + 1, 0))
        wait(b, jg)
        slot = slot_of(b, jg)

        member = mask_ref[:, jg * span:(jg + 1) * span] > 0.5
        s = jnp.concatenate([_dot(qa, kbuf[slot, g].astype(BF16)) for g in range(g_pages)], axis=-1)
        s = jnp.where(member, s * A_SCALE, -jnp.inf)
        _online_update(s, lambda p: sum(_dot_nt(page(p, g), vbuf[slot, g].astype(BF16))
                                        for g in range(g_pages)), ma_ref, la_ref, acca_ref)

        cks = [cbuf[slot, g].astype(BF16) for g in range(g_pages)]
        s = jnp.concatenate([_dot_nt(ql, cks[g]) + _dot(qr, rbuf[slot, g].astype(BF16))
                             for g in range(g_pages)], axis=-1) * C_SCALE
        _online_update(s, lambda p: sum(_dot(page(p, g), cks[g]) for g in range(g_pages)),
                       mc_ref, lc_ref, accc_ref)

    rnd = lambda x: x.astype(BF16).astype(F32)
    member = mask_ref[:, n_pages * PAGE_SIZE:n_pages * PAGE_SIZE + 1] > 0.5
    s = jnp.sum(rnd(qa_ref[...]) * rnd(kn_ref[...]), axis=1, keepdims=True) * A_SCALE
    s = jnp.where(member, s, -jnp.inf)
    _online_update_one(s, rnd(vn_ref[...]), ma_ref, la_ref, acca_ref)
    cn = rnd(cn_ref[...])
    s = (jnp.sum(ql.astype(F32) * cn, axis=1, keepdims=True)
         + jnp.sum(rnd(qr_ref[...]) * rnd(rn_ref[...]), axis=1, keepdims=True)) * C_SCALE
    _online_update_one(s, cn, mc_ref, lc_ref, accc_ref)
    o = acca_ref[...] / la_ref[...]
    first_group = lax.broadcasted_iota(jnp.int32, oa_ref.shape, 0) < A_HEADS // A_KV_HEADS
    oa_ref[...] = jnp.where(first_group, o[:, :A_HEAD_DIM], o[:, A_HEAD_DIM:])
    oc_ref[...] = accc_ref[...] / lc_ref[...]


def _sample_attn(pt_flat, mask3, qa, ql, qr, kn, vn, cn, rn, cache_k, cache_v, cache_c, cache_r,
                 layer, n_pages, g_pages):
    bd = qa.shape[0]
    ns = mask3.shape[-1]

    kv_rows = A_KV_HEADS * A_HEAD_DIM
    per_b = lambda r, w: pl.BlockSpec((None, r, w), lambda b, pt: (b, 0, 0))
    in_hbm = pl.BlockSpec(memory_space=pl.ANY)
    slots = lambda r, w: pltpu.VMEM((2, g_pages, r, w), F32)
    grid_spec = pltpu.PrefetchScalarGridSpec(
        num_scalar_prefetch=1,
        grid=(bd,),
        in_specs=[per_b(1, ns), per_b(A_HEADS, LANES), per_b(C_HEADS, C_KV_RANK), per_b(C_HEADS, C_ROPE),
                  per_b(1, LANES), per_b(1, LANES), per_b(1, C_KV_RANK), per_b(1, C_ROPE),
                  in_hbm, in_hbm, in_hbm, in_hbm],
        out_specs=[per_b(A_HEADS, A_HEAD_DIM), per_b(C_HEADS, C_KV_RANK)],
        scratch_shapes=[slots(kv_rows, PAGE_SIZE), slots(kv_rows, PAGE_SIZE),
                        slots(PAGE_SIZE, C_KV_RANK), slots(C_ROPE, PAGE_SIZE),
                        pltpu.SemaphoreType.DMA((2, 4)),
                        pltpu.VMEM((A_HEADS, 1), F32), pltpu.VMEM((A_HEADS, 1), F32),
                        pltpu.VMEM((A_HEADS, LANES), F32),
                        pltpu.VMEM((C_HEADS, 1), F32), pltpu.VMEM((C_HEADS, 1), F32),
                        pltpu.VMEM((C_HEADS, C_KV_RANK), F32)],
    )
    return pl.pallas_call(
        functools.partial(_sample_attn_kernel, layer=layer, g_pages=g_pages, n_pages=n_pages),
        grid_spec=grid_spec,
        out_shape=[jax.ShapeDtypeStruct((bd, A_HEADS, A_HEAD_DIM), F32),
                   jax.ShapeDtypeStruct((bd, C_HEADS, C_KV_RANK), F32)],
        compiler_params=_cparams(("arbitrary",)),
        name="sample_attn",
    )(pt_flat, mask3, qa, ql, qr, kn, vn, cn, rn, cache_k, cache_v, cache_c, cache_r)


def _uv_kernel(o_ref, wuv_ref, y_ref):
    w = 2 * C_KV_RANK
    outs = [_dot(o_ref[:, w * p:w * (p + 1)].astype(BF16), wuv_ref[p]) for p in range(C_HEADS // 2)]
    y_ref[...] = jnp.concatenate(outs, axis=-1).astype(y_ref.dtype)


def _uv_proj(o_lat, wuvp):
    m = o_lat.shape[0]
    return pl.pallas_call(
        _uv_kernel,
        grid=(1,),
        in_specs=[pl.BlockSpec(o_lat.shape, lambda i: (0, 0)),
                  pl.BlockSpec(wuvp.shape, lambda i: (0, 0, 0))],
        out_specs=pl.BlockSpec((m, C_WIDTH), lambda i: (0, 0)),
        out_shape=jax.ShapeDtypeStruct((m, C_WIDTH), BF16),
        compiler_params=_cparams(("arbitrary",)),
        name="uv_proj",
    )(o_lat, wuvp)


def _rope_tables(pos):
    half = C_ROPE // 2
    freqs = ROPE_THETA ** (-jnp.arange(half, dtype=F32) / half)
    ang = pos.astype(F32)[:, None] * freqs[None, :]
    return jnp.tile(jnp.cos(ang), (1, C_HEADS)), jnp.tile(jnp.sin(ang), (1, C_HEADS))


def _layer_weights(l, w_in, w_out_a, conv_b, w_out_b, g_cq, w_uq, g_ckv, w_uk, w_uv, w_out_c,
                   conv_d_w, conv_d_b, w_rg, b_rg, w_ig, b_ig, lru_lambda, w_out_d, w_o,
                   w_ffn_gate, w_ffn_up, w_ffn_down):
    lw = {}
    lw["w_in"] = _relayout_w_in(jnp.transpose(w_in, (0, 2, 1))[l])
    uq = w_uq[l].reshape(C_Q_RANK, C_HEADS, C_NOPE + C_ROPE)
    half = C_ROPE // 2
    lw["w_uq"] = jnp.concatenate(
        [uq[:, :, :C_NOPE].reshape(C_Q_RANK, -1),
         uq[:, :, C_NOPE:C_NOPE + half].reshape(C_Q_RANK, -1),
         uq[:, :, C_NOPE + half:].reshape(C_Q_RANK, -1)], axis=1).astype(BF16)
    ukt = jnp.transpose(w_uk[l], (1, 2, 0))
    zk = jnp.zeros((C_NOPE, C_KV_RANK), F32)
    lw["w_uk"] = jnp.stack([
        jnp.concatenate([jnp.concatenate([ukt[2 * p], zk], axis=1),
                         jnp.concatenate([zk, ukt[2 * p + 1]], axis=1)], axis=0)
        for p in range(C_HEADS // 2)]).astype(BF16)
    uv = jnp.transpose(w_uv[l], (1, 0, 2))
    zv = jnp.zeros((C_KV_RANK, C_VDIM), F32)
    lw["w_uv"] = jnp.stack([
        jnp.concatenate([jnp.concatenate([uv[2 * p], zv], axis=1),
                         jnp.concatenate([zv, uv[2 * p + 1]], axis=1)], axis=0)
        for p in range(C_HEADS // 2)]).astype(BF16)
    eye = jnp.eye(D_BLOCKS, dtype=F32)
    bd = lambda w: (eye[:, None, :, None] * w[:, :, None, :]).reshape(D_WIDTH, D_WIDTH).astype(BF16)
    lw["w_rg"] = bd(w_rg[l])
    lw["w_ig"] = bd(w_ig[l])
    row = lambda v: v[l][None, :]
    lw["b_rg"], lw["b_ig"], lw["lam"], lw["conv_d_b"] = row(b_rg), row(b_ig), row(lru_lambda), row(conv_d_b)
    lw["g_cq"], lw["g_ckv"] = row(g_cq), row(g_ckv)
    lw["conv_b"], lw["conv_d_w"] = conv_b[l], conv_d_w[l]
    for name, w in (("w_out_a", w_out_a), ("w_out_b", w_out_b), ("w_out_c", w_out_c),
                    ("w_out_d", w_out_d), ("w_o", w_o), ("w_ffn_gate", w_ffn_gate),
                    ("w_ffn_up", w_ffn_up), ("w_ffn_down", w_ffn_down)):
        lw[name] = w[l].astype(BF16)
    return lw


def _zcols(z, name, lo, hi):
    o = _DST[name][0]
    return z[..., o + lo:o + hi]


def _state_from_z(z):
    hd = A_HEAD_DIM
    ak = jnp.concatenate([_zcols(z, "ak", 0, hd), _zcols(z, "ak", 2 * hd, 3 * hd)], axis=-1)
    av = jnp.concatenate([_zcols(z, "av", 0, hd), _zcols(z, "av", 2 * hd, 3 * hd)], axis=-1)
    ik = _zcols(z, "ik", 0, IDX_DIM)
    return ak, av, ik


def _krope_state(kr):
    half = C_ROPE // 2
    return jnp.concatenate([kr[:, :half], kr[:, LANES:LANES + half]], axis=-1)


def _prompt_layer(x, lw, gains, bn, s_len, cos, sin):
    g_mix_pre, g_mix_post, g_ffn_pre, g_ffn_post = gains
    m = bn * s_len
    z = _inproj(x, g_mix_pre, lw["w_in"], tm=1024)
    z3 = z.reshape(bn, s_len, N_PROJ)
    tm_c = min(512, s_len)
    nb = s_len // tm_c
    q, kcat, ckvn, kr, _ = _cprep(z, cos, sin, lambda i: (i % nb, 0), lw["g_cq"], lw["g_ckv"],
                                  lw["w_uq"], lw["w_uk"], tm=tm_c)
    yc = _mla_prompt(q, kcat, lw["w_uv"], bn, s_len)
    ya = _sparse_prompt(z3, min(TOPK_MAX, s_len // 4)).reshape(m, A_WIDTH)
    yb, b_new = _bconv_prompt(z3, lw["conv_b"])
    yd, d_new, h_new = _rglru_prompt(z3, lw["conv_d_w"], lw["conv_d_b"], lw["w_rg"], lw["b_rg"],
                                     lw["w_ig"], lw["b_ig"], lw["lam"])
    x = _merge(ya, yb.reshape(m, B_WIDTH), yc, yd.reshape(m, D_WIDTH), z, x,
               lw["w_out_a"], lw["w_out_b"], lw["w_out_c"], lw["w_out_d"], lw["w_o"], g_mix_post, tm=512)
    x = _ffn(x, g_ffn_pre, lw["w_ffn_gate"], lw["w_ffn_up"], lw["w_ffn_down"], g_ffn_post, tm=1024)
    ak, av, ik = _state_from_z(z3)
    st = (ak.reshape(bn, s_len, A_KV_HEADS, A_HEAD_DIM), av.reshape(bn, s_len, A_KV_HEADS, A_HEAD_DIM),
          ik, ckvn.reshape(bn, s_len, C_KV_RANK), _krope_state(kr).reshape(bn, s_len, C_ROPE),
          b_new, d_new, h_new.reshape(bn, D_WIDTH))
    return x, st


def _sample_layer(x, lw, gains, layer, caches, states, pt_flat, n_pages, cos, sin, g_pages):
    g_mix_pre, g_mix_post, g_ffn_pre, g_ffn_post = gains
    cache_k, cache_v, cache_ki, cache_c, cache_r = caches
    sb, sd, h0 = states
    bd = x.shape[0]
    z = _inproj(x, g_mix_pre, lw["w_in"], tm=bd)
    q, _, ckvn, kr, qrope = _cprep(z, cos, sin, lambda i: (0, 0), lw["g_cq"], lw["g_ckv"],
                                   lw["w_uq"], lw["w_uk"], tm=bd)
    ak, av, ik = _state_from_z(z)
    krope = _krope_state(kr)

    iq3 = jnp.pad(_zcols(z, "iq", 0, 256).reshape(bd, IDX_HEADS, IDX_DIM),
                  ((0, 0), (0, IDX_ROWS - IDX_HEADS), (0, 0)))
    iw3 = jnp.pad(_zcols(z, "iw", 0, IDX_HEADS), ((0, 0), (0, IDX_ROWS - IDX_HEADS)))[:, :, None]
    scores = _sample_scores(pt_flat, iq3, iw3, ik[:, None, :], cache_ki, layer, n_pages, g_pages[0])
    total = n_pages * PAGE_SIZE + 1
    mask = _sample_select(scores.reshape(bd, -1), min(TOPK_MAX, total // 4), total)

    aq = _zcols(z, "aq", 0, A_WIDTH).reshape(bd, A_KV_HEADS, A_HEADS // A_KV_HEADS, A_HEAD_DIM)
    zq = jnp.zeros_like(aq[:, 0])
    qa = jnp.concatenate([jnp.concatenate([aq[:, 0], zq], axis=-1),
                          jnp.concatenate([zq, aq[:, 1]], axis=-1)], axis=1)
    ql = jnp.transpose(q[:, :, :C_KV_RANK], (1, 0, 2))
    half = C_ROPE // 2
    qr = jnp.concatenate([qrope[:, :LANES].reshape(bd, C_HEADS, half),
                          qrope[:, LANES:].reshape(bd, C_HEADS, half)], axis=-1)
    oa, oc = _sample_attn(pt_flat, mask[:, None, :], qa, ql, qr, ak[:, None, :], av[:, None, :],
                          ckvn[:, None, :], krope[:, None, :], cache_k, cache_v, cache_c, cache_r,
                          layer, n_pages, g_pages[1])
    ya = oa.reshape(bd, A_WIDTH).astype(BF16)
    yc = _uv_proj(oc.reshape(bd, C_HEADS * C_KV_RANK), lw["w_uv"])

    yb, yd, b_new, d_new, h_new = _sample_bd(
        z, sb.reshape(bd, -1), sd.reshape(bd, -1), h0, lw["conv_b"], lw["conv_d_w"], lw["conv_d_b"],
        lw["w_rg"], lw["b_rg"], lw["w_ig"], lw["b_ig"], lw["lam"])
    x = _merge(ya, yb, yc, yd, z, x, lw["w_out_a"], lw["w_out_b"], lw["w_out_c"], lw["w_out_d"],
               lw["w_o"], g_mix_post, tm=bd)
    x = _ffn(x, g_ffn_pre, lw["w_ffn_gate"], lw["w_ffn_up"], lw["w_ffn_down"], g_ffn_post, tm=bd)
    st = (ak.reshape(bd, 1, A_KV_HEADS, A_HEAD_DIM), av.reshape(bd, 1, A_KV_HEADS, A_HEAD_DIM),
          ik[:, None, :], ckvn[:, None, :], krope[:, None, :],
          b_new.reshape(bd, B_CONV - 1, B_WIDTH), d_new.reshape(bd, D_CONV - 1, D_WIDTH), h_new)
    return x, st


def kernel(x_prompt, x_sample, cache_a_k, cache_a_v, cache_a_kidx, cache_c_kv, cache_c_krope, state_b_conv, state_d_conv, state_d_h, page_table, g_mix_pre, g_mix_post, g_ffn_pre, g_ffn_post, w_in, w_out_a, conv_b, w_out_b, g_cq, w_uq, g_ckv, w_uk, w_uv, w_out_c, conv_d_w, conv_d_b, w_rg, b_rg, w_ig, b_ig, lru_lambda, w_out_d, w_o, w_ffn_gate, w_ffn_up, w_ffn_down):
    bn, s_len, _ = x_prompt.shape
    bd, t_len, _ = x_sample.shape
    assert t_len == 1 and s_len % Q_BLOCK == 0
    depth = w_in.shape[0]
    n_pages = page_table.shape[1]
    n_pool = cache_a_k.shape[1]
    past = n_pages * PAGE_SIZE
    largest_group = lambda cap: max(g for g in range(1, cap + 1) if n_pages % g == 0)
    g_pages = (largest_group(PAGES_PER_SCORE_STEP), largest_group(PAGES_PER_ATTN_STEP))

    cos_p, sin_p = _rope_tables(jnp.arange(s_len))
    cos_s, sin_s = _rope_tables(jnp.full((bd,), past))
    pt_flat = page_table.reshape(-1)
    kv_t = lambda c: jnp.transpose(c, (0, 1, 3, 4, 2)).reshape(
        depth, n_pool, A_KV_HEADS * A_HEAD_DIM, PAGE_SIZE)
    caches = (kv_t(cache_a_k), kv_t(cache_a_v), jnp.transpose(cache_a_kidx, (0, 1, 3, 2)),
              cache_c_kv, jnp.transpose(cache_c_krope, (0, 1, 3, 2)))

    xp = x_prompt.reshape(bn * s_len, D_MODEL)
    xs = x_sample.reshape(bd, D_MODEL)
    p_states, s_states = [], []
    for l in range(depth):
        lw = _layer_weights(l, w_in, w_out_a, conv_b, w_out_b, g_cq, w_uq, g_ckv, w_uk, w_uv, w_out_c,
                            conv_d_w, conv_d_b, w_rg, b_rg, w_ig, b_ig, lru_lambda, w_out_d, w_o,
                            w_ffn_gate, w_ffn_up, w_ffn_down)
        gains = tuple(g[l][None, :] for g in (g_mix_pre, g_mix_post, g_ffn_pre, g_ffn_post))
        xs, st_s = _sample_layer(xs, lw, gains, l, caches,
                                 (state_b_conv[l], state_d_conv[l], state_d_h[l]),
                                 pt_flat, n_pages, cos_s, sin_s, g_pages)
        xp, st_p = _prompt_layer(xp, lw, gains, bn, s_len, cos_p, sin_p)
        p_states.append(st_p)
        s_states.append(st_s)
    p_out = [jnp.stack([st[j] for st in p_states]) for j in range(8)]
    s_out = [jnp.stack([st[j] for st in s_states]) for j in range(8)]
    return (xp.reshape(bn, s_len, D_MODEL), xs.reshape(bd, 1, D_MODEL), *p_out, *s_out)
```

```python
import functools

import numpy as np
import jax
import jax.numpy as jnp
from jax import lax
from jax.experimental import pallas as pl
from jax.experimental.pallas import tpu as pltpu

D_MODEL = 1024
PAGE_SIZE = 128
A_HEADS = 8
A_KV_HEADS = 2
A_HEAD_DIM = 64
A_WIDTH = A_HEADS * A_HEAD_DIM
A_SCALE = A_HEAD_DIM ** -0.5
IDX_HEADS = 4
IDX_DIM = 64
TOPK_MAX = 256
B_WIDTH = 512
B_CONV = 3
C_HEADS = 8
C_NOPE = 64
C_ROPE = 32
C_VDIM = 64
C_Q_RANK = 256
C_KV_RANK = 256
C_WIDTH = C_HEADS * C_VDIM
C_SCALE = (C_NOPE + C_ROPE) ** -0.5
ROPE_THETA = 10000.0
D_WIDTH = 512
D_BLOCKS = 8
D_BLOCK = D_WIDTH // D_BLOCKS
D_CONV = 4
LRU_C = 8.0
N_BRANCH = 4
D_FF = ((8 * D_MODEL + 3 * 256 - 1) // (3 * 256)) * 256
Q_BLOCK = 128
EPS = 1e-6

LANES = 128
VMEM_LIMIT = 56 * 1024 * 1024
NEG_BIG = -1e30

BF16 = jnp.bfloat16
F32 = jnp.float32

_SRC = {}
_o = 0
for _n, _w in (("aq", A_WIDTH), ("ak", 128), ("av", 128), ("iq", 256), ("ik", 64), ("iw", 4),
               ("gb", 512), ("gc", 512), ("xb", 512), ("cq", 256), ("ckv", 256), ("ckr", 32),
               ("dg", 512), ("dx", 512), ("gt", 4096)):
    _SRC[_n] = (_o, _w)
    _o += _w
N_IN = _o

_DST = {}
_o = 0
for _n, _w in (("aq", 512), ("gb", 512), ("gc", 512), ("xb", 512), ("dg", 512), ("dx", 512),
               ("gt", 4096), ("iq", 256), ("cq", 256), ("ckv", 256), ("ckr", 256),
               ("ak", 256), ("av", 256), ("ik", 128), ("iw", 128)):
    _DST[_n] = (_o, _w)
    _o += _w
N_PROJ = _o
PROJ_TN = 1280


def _blk(name, width):
    off = _DST[name][0]
    assert off % width == 0
    return off // width


def _cparams(sem, vmem=VMEM_LIMIT):
    return pltpu.CompilerParams(dimension_semantics=sem, vmem_limit_bytes=vmem)


def _relayout_w_in(wt):
    def src(name, lo=0, hi=None):
        o, wd = _SRC[name]
        hi = wd if hi is None else hi
        return wt[o + lo:o + hi]

    zeros = lambda n: jnp.zeros((n, wt.shape[1]), wt.dtype)
    parts = [src("aq"), src("gb"), src("gc"), src("xb"), src("dg"), src("dx"), src("gt"),
             src("iq"), src("cq"), src("ckv"),
             jnp.tile(src("ckr", 0, 16), (8, 1)), jnp.tile(src("ckr", 16, 32), (8, 1)),
             src("ak", 0, 64), src("ak", 0, 64), src("ak", 64, 128), src("ak", 64, 128),
             src("av", 0, 64), src("av", 0, 64), src("av", 64, 128), src("av", 64, 128),
             src("ik"), src("ik"),
             src("iw"), zeros(124)]
    out = jnp.concatenate(parts, axis=0)
    assert out.shape[0] == N_PROJ
    return out.astype(BF16)


def _rms(x, g):
    inv = lax.rsqrt(jnp.mean(x * x, axis=-1, keepdims=True) + EPS)
    return (x * inv) * g


def _dot(a, b):
    return jnp.dot(a, b, preferred_element_type=F32)


def _dot_nt(a, b):
    return lax.dot_general(a, b, (((1,), (1,)), ((), ())), preferred_element_type=F32)


def _inproj_kernel(x_ref, g_ref, w_ref, o_ref, h_ref):
    @pl.when(pl.program_id(1) == 0)
    def _():
        h_ref[...] = _rms(x_ref[...], g_ref[...]).astype(BF16)

    o_ref[...] = _dot_nt(h_ref[...], w_ref[...])


def _inproj(x, g, w, tm):
    m = x.shape[0]
    tm = min(tm, m)
    return pl.pallas_call(
        _inproj_kernel,
        grid=(m // tm, N_PROJ // PROJ_TN),
        in_specs=[pl.BlockSpec((tm, D_MODEL), lambda i, j: (i, 0)),
                  pl.BlockSpec((1, D_MODEL), lambda i, j: (0, 0)),
                  pl.BlockSpec((PROJ_TN, D_MODEL), lambda i, j: (j, 0))],
        out_specs=pl.BlockSpec((tm, PROJ_TN), lambda i, j: (i, j)),
        out_shape=jax.ShapeDtypeStruct((m, N_PROJ), F32),
        scratch_shapes=[pltpu.VMEM((tm, D_MODEL), BF16)],
        compiler_params=_cparams(("parallel", "arbitrary")),
        name="in_proj",
    )(x, g, w)


def _cprep_kernel(cq_ref, ckv_ref, ckr_ref, cos_ref, sin_ref, gq_ref, gkv_ref, wuq_ref, wuk_ref,
                  q_ref, kcat_ref, ckvn_ref, kr_ref, qr_ref):
    cos = cos_ref[...]
    sin = sin_ref[...]
    cqn = _rms(cq_ref[...], gq_ref[...]).astype(BF16)
    q = _dot(cqn, wuq_ref[...])
    nope = C_HEADS * C_NOPE
    r1 = q[:, nope:nope + LANES]
    r2 = q[:, nope + LANES:nope + 2 * LANES]
    o1 = r1 * cos - r2 * sin
    o2 = r1 * sin + r2 * cos
    qr_ref[...] = jnp.concatenate([o1, o2], axis=-1)
    head_of_lane = lax.broadcasted_iota(jnp.int32, o1.shape, 1) // (C_ROPE // 2)
    for p in range(C_HEADS // 2):
        ql = _dot(q[:, LANES * p:LANES * (p + 1)].astype(BF16), wuk_ref[p])
        for e in range(2):
            h = 2 * p + e
            sel = head_of_lane == h
            q_ref[h] = jnp.concatenate(
                [ql[:, C_KV_RANK * e:C_KV_RANK * (e + 1)],
                 jnp.where(sel, o1, 0.0), jnp.where(sel, o2, 0.0)], axis=-1).astype(BF16)
    ckvn = _rms(ckv_ref[...], gkv_ref[...])
    ckvn_ref[...] = ckvn
    c = ckr_ref[...]
    c1 = c[:, :LANES]
    c2 = c[:, LANES:]
    k1 = c1 * cos - c2 * sin
    k2 = c1 * sin + c2 * cos
    kr_ref[...] = jnp.concatenate([k1, k2], axis=-1)
    kcat_ref[...] = jnp.concatenate([ckvn, k1, k2], axis=-1).astype(BF16)


def _cprep(z, cos, sin, tab_map, g_cq, g_ckv, wuq, wukp, tm):
    m = z.shape[0]
    tm = min(tm, m)
    qw = C_KV_RANK + 2 * LANES
    const2 = lambda i: (0, 0)
    return pl.pallas_call(
        _cprep_kernel,
        grid=(m // tm,),
        in_specs=[pl.BlockSpec((tm, 256), lambda i: (i, _blk("cq", 256))),
                  pl.BlockSpec((tm, 256), lambda i: (i, _blk("ckv", 256))),
                  pl.BlockSpec((tm, 256), lambda i: (i, _blk("ckr", 256))),
                  pl.BlockSpec((tm, LANES), tab_map),
                  pl.BlockSpec((tm, LANES), tab_map),
                  pl.BlockSpec((1, C_Q_RANK), const2),
                  pl.BlockSpec((1, C_KV_RANK), const2),
                  pl.BlockSpec(wuq.shape, const2),
                  pl.BlockSpec(wukp.shape, lambda i: (0, 0, 0))],
        out_specs=[pl.BlockSpec((C_HEADS, tm, qw), lambda i: (0, i, 0)),
                   pl.BlockSpec((tm, qw), lambda i: (i, 0)),
                   pl.BlockSpec((tm, C_KV_RANK), lambda i: (i, 0)),
                   pl.BlockSpec((tm, 2 * LANES), lambda i: (i, 0)),
                   pl.BlockSpec((tm, 2 * LANES), lambda i: (i, 0))],
        out_shape=[jax.ShapeDtypeStruct((C_HEADS, m, qw), BF16),
                   jax.ShapeDtypeStruct((m, qw), BF16),
                   jax.ShapeDtypeStruct((m, C_KV_RANK), F32),
                   jax.ShapeDtypeStruct((m, 2 * LANES), F32),
                   jax.ShapeDtypeStruct((m, 2 * LANES), F32)],
        compiler_params=_cparams(("parallel",)),
        name="c_prep",
    )(z, z, z, cos, sin, g_cq, g_ckv, wuq, wukp)


MLA_CHUNK_ROWS = 256
MLA_KEY_BLOCK = 512
MLA_QUERY_BLOCK = 512


def _mla_kernel(q_ref, k_ref, wuv_ref, o_ref, m_ref, l_ref, acc_ref, *, tk, nkb, qb):
    i = pl.program_id(1)
    j = pl.program_id(2)
    chunk_heads = max(1, MLA_CHUNK_ROWS // qb)
    chunk_rows = chunk_heads * qb

    @pl.when(j == 0)
    def _():
        m_ref[...] = jnp.full(m_ref.shape, NEG_BIG, F32)
        l_ref[...] = jnp.zeros(l_ref.shape, F32)
        acc_ref[...] = jnp.zeros(acc_ref.shape, F32)

    def step(masked):
        k = k_ref[...]
        v = k[:, :C_KV_RANK]
        n_chunk = C_HEADS // chunk_heads

        def scores(c):
            q = q_ref[c * chunk_heads:(c + 1) * chunk_heads].reshape(chunk_rows, q_ref.shape[-1])
            return _dot_nt(q, k) * C_SCALE

        lane_tiles = lambda x: [x[:, LANES * u:LANES * (u + 1)] for u in range(x.shape[1] // LANES)]
        s_next = scores(0)
        for c in range(n_chunk):
            r0, r1 = c * chunk_rows, (c + 1) * chunk_rows
            s = s_next
            if c + 1 < n_chunk:
                s_next = scores(c + 1)
            if masked:
                t = i * qb + lax.broadcasted_iota(jnp.int32, s.shape, 0) % qb
                col = j * tk + lax.broadcasted_iota(jnp.int32, s.shape, 1)
                s = jnp.where(col <= t, s, -jnp.inf)
            m_old = m_ref[r0:r1, :]
            m_new = jnp.maximum(m_old, jnp.max(s, axis=1, keepdims=True))
            alpha = jnp.exp(m_old - m_new)
            p_tiles = [jnp.exp(st - m_new) for st in lane_tiles(s)]
            l_ref[r0:r1, :] = alpha * l_ref[r0:r1, :] + sum(p_tiles)
            pv = _dot(jnp.concatenate(p_tiles, axis=-1).astype(BF16), v)
            alpha_wide = jnp.concatenate([alpha] * (C_KV_RANK // LANES), axis=-1)
            acc_ref[r0:r1, :] = alpha_wide * acc_ref[r0:r1, :] + pv
            m_ref[r0:r1, :] = m_new

    first_row = i * qb
    fully_visible = j * tk + tk - 1 <= first_row
    pl.when(fully_visible)(functools.partial(step, False))
    pl.when(jnp.logical_and(jnp.logical_not(fully_visible), j * tk <= first_row + qb - 1))(
        functools.partial(step, True))

    @pl.when(j == nkb - 1)
    def _():
        o = acc_ref[...] / jnp.sum(l_ref[...], axis=1, keepdims=True)
        outs = []
        for p in range(C_HEADS // 2):
            pair = jnp.concatenate([o[2 * p * qb:(2 * p + 1) * qb],
                                    o[(2 * p + 1) * qb:(2 * p + 2) * qb]], axis=-1)
            outs.append(_dot(pair.astype(BF16), wuv_ref[p]))
        o_ref[...] = jnp.concatenate(outs, axis=-1).astype(o_ref.dtype)


def _mla_prompt(q, kcat, wuvp, bn, s_len):
    qb = min(MLA_QUERY_BLOCK, s_len)
    nq = s_len // qb
    tk = min(MLA_KEY_BLOCK, s_len)
    nkb = s_len // tk
    qw = q.shape[-1]

    def k_map(b, i, j):
        last = (i * qb + qb - 1) // tk
        return (b * nkb + jnp.minimum(j, last), 0)

    return pl.pallas_call(
        functools.partial(_mla_kernel, tk=tk, nkb=nkb, qb=qb),
        grid=(bn, nq, nkb),
        in_specs=[pl.BlockSpec((C_HEADS, qb, qw), lambda b, i, j: (0, b * nq + i, 0)),
                  pl.BlockSpec((tk, qw), k_map),
                  pl.BlockSpec(wuvp.shape, lambda b, i, j: (0, 0, 0))],
        out_specs=pl.BlockSpec((qb, C_WIDTH), lambda b, i, j: (b * nq + i, 0)),
        out_shape=jax.ShapeDtypeStruct((bn * s_len, C_WIDTH), BF16),
        scratch_shapes=[pltpu.VMEM((C_HEADS * qb, LANES), F32),
                        pltpu.VMEM((C_HEADS * qb, LANES), F32),
                        pltpu.VMEM((C_HEADS * qb, C_KV_RANK), F32)],
        compiler_params=_cparams(("parallel", "parallel", "arbitrary")),
        name="mla_prompt",
    )(q, kcat, wuvp)


def _rowsum(x):
    return jnp.sum(x, axis=1, keepdims=True)


def _topk_select(xs, allowed, col, k, nbits, n_bisect):
    kf = float(k)
    inf = jnp.float32(jnp.inf)
    row_min = jnp.min(jnp.where(allowed, xs, inf), axis=1, keepdims=True)
    row_max = jnp.max(xs, axis=1, keepdims=True)
    n_allowed = _rowsum(jnp.where(allowed, 1.0, 0.0))

    def bisect(_, c):
        low, high, has_low = c
        mid = low + (high - low) * 0.5
        up = _rowsum(jnp.where(xs > mid, 1.0, 0.0)) >= kf
        return jnp.where(up, mid, low), jnp.where(up, high, mid), jnp.where(up, 1.0, has_low)

    low, high, has_low = lax.fori_loop(0, n_bisect, bisect, (row_min, row_max, jnp.zeros_like(row_min)))
    lo0 = jnp.min(jnp.where(xs > low, xs, inf), axis=1, keepdims=True)
    lo0 = jnp.where((has_low > 0.5) & (n_allowed > kf), lo0, row_min)
    ub0 = jnp.max(jnp.where(xs > high, -inf, xs), axis=1, keepdims=True)
    ub0 = jnp.where(n_allowed > kf, ub0, row_min)

    def cond(c):
        lo, ub = c
        return jnp.max(jnp.where(lo < ub, 1.0, 0.0)) > 0.0

    def body(c):
        lo, ub = c
        mid = lo + (ub - lo) * 0.5
        mid = jnp.where((mid >= lo) & (mid < ub), mid, lo)
        gt = xs > mid
        cnt = _rowsum(jnp.where(gt, 1.0, 0.0))
        vmin = jnp.min(jnp.where(gt, xs, inf), axis=1, keepdims=True)
        vmax = jnp.max(jnp.where(gt, -inf, xs), axis=1, keepdims=True)
        active = lo < ub
        up = cnt >= kf
        lo = jnp.where(active & up, vmin, lo)
        ub = jnp.where(active & jnp.logical_not(up), vmax, ub)
        return lo, ub

    v, _ = lax.while_loop(cond, body, (lo0, ub0))
    gtv = xs > v
    need = kf - _rowsum(jnp.where(gtv, 1.0, 0.0))
    tie = allowed & (xs == v)
    n_tie = _rowsum(jnp.where(tie, 1.0, 0.0))

    def search(_):
        def step(it, c):
            cand = c + jnp.left_shift(jnp.int32(1), nbits - 1 - it)
            f = _rowsum(jnp.where(tie & (col < cand), 1.0, 0.0))
            return jnp.where(f < need, cand, c)
        return lax.fori_loop(0, nbits, step, jnp.zeros(v.shape, jnp.int32))

    def no_search(_):
        return jnp.full(v.shape, (1 << nbits) - 1, jnp.int32)

    any_over = jnp.max(jnp.where(n_tie > need, 1.0, 0.0)) > 0.0
    cut = lax.cond(any_over, search, no_search, 0)
    return gtv | (tie & (col <= cut))


KEY_CLASS = 256
N_BISECT_PROMPT = 14
N_BISECT_SAMPLE = 16


def _sparse_prompt_kernel(q_ref, iq_ref, iw_ref, kd_ref, vd_ref, ikd_ref, o_ref, kb_ref, vb_ref, ib_ref,
                          *, topk, s_len):
    i = pl.program_id(1)

    @pl.when(i == 0)
    def _():
        ib_ref[...] = ikd_ref[...].astype(BF16)
        for g in range(A_KV_HEADS):
            kb_ref[g] = kd_ref[:, LANES * g:LANES * (g + 1)].astype(BF16)
            vb_ref[g] = vd_ref[:, LANES * g:LANES * (g + 1)].astype(BF16)

    first_half = lax.broadcasted_iota(jnp.int32, (Q_BLOCK, LANES), 1) < (LANES // 2)

    def half(x, pair, e):
        blk = x[:, LANES * pair:LANES * (pair + 1)]
        return jnp.where(first_half if e == 0 else jnp.logical_not(first_half), blk, 0.0).astype(BF16)

    def body(width):
        q = q_ref[...] * A_SCALE
        iq = iq_ref[...]
        iw = iw_ref[...]
        ikd = ib_ref[0:width, :]
        score = None
        for h in range(IDX_HEADS):
            rel = jnp.maximum(_dot_nt(half(iq, h // 2, h % 2), ikd), 0.0)
            term = iw[:, h:h + 1] * rel
            score = term if score is None else score + term
        t = i * Q_BLOCK + lax.broadcasted_iota(jnp.int32, score.shape, 0)
        col = lax.broadcasted_iota(jnp.int32, score.shape, 1)
        allowed = col <= t
        xs = jnp.where(allowed, score + 0.0, -jnp.inf)
        nbits = max(1, (width - 1).bit_length())
        sel = _topk_select(xs, allowed, col, topk, nbits, N_BISECT_PROMPT)
        bias = jnp.where(sel, 0.0, -jnp.inf)

        rep = A_HEADS // A_KV_HEADS
        outs = []
        for g in range(A_KV_HEADS):
            kd = kb_ref[g, 0:width, :]
            vd = vb_ref[g, 0:width, :]
            heads = [g * rep + r for r in range(rep)]
            s_all = _dot_nt(jnp.concatenate([half(q, h // 2, h % 2) for h in heads], axis=0), kd)
            ps, ls = [], []
            for r in range(rep):
                s = s_all[r * Q_BLOCK:(r + 1) * Q_BLOCK] + bias
                m = jnp.max(s, axis=1, keepdims=True)
                p = jnp.exp(s - m)
                ls.append(_rowsum(p))
                ps.append(p.astype(BF16))
            o_all = _dot(jnp.concatenate(ps, axis=0), vd)
            for r in range(rep):
                outs.append(o_all[r * Q_BLOCK:(r + 1) * Q_BLOCK] / ls[r])
        pairs = [jnp.where(first_half, outs[2 * j], outs[2 * j + 1]) for j in range(A_HEADS // 2)]
        o_ref[...] = jnp.concatenate(pairs, axis=-1).astype(o_ref.dtype)

    per_class = KEY_CLASS // Q_BLOCK
    n_class = max(1, s_len // KEY_CLASS)
    for c in range(n_class):
        width = min(s_len, (c + 1) * KEY_CLASS)
        pl.when(i // per_class == c)(functools.partial(body, width))


def _sparse_prompt(z3, topk):
    bn, s_len, _ = z3.shape
    nq = s_len // Q_BLOCK
    return pl.pallas_call(
        functools.partial(_sparse_prompt_kernel, topk=topk, s_len=s_len),
        grid=(bn, nq),
        in_specs=[pl.BlockSpec((None, Q_BLOCK, 512), lambda b, i: (b, i, _blk("aq", 512))),
                  pl.BlockSpec((None, Q_BLOCK, 256), lambda b, i: (b, i, _blk("iq", 256))),
                  pl.BlockSpec((None, Q_BLOCK, 128), lambda b, i: (b, i, _blk("iw", 128))),
                  pl.BlockSpec((None, s_len, 256), lambda b, i: (b, 0, _blk("ak", 256))),
                  pl.BlockSpec((None, s_len, 256), lambda b, i: (b, 0, _blk("av", 256))),
                  pl.BlockSpec((None, s_len, 128), lambda b, i: (b, 0, _blk("ik", 128)))],
        out_specs=pl.BlockSpec((None, Q_BLOCK, A_WIDTH), lambda b, i: (b, i, 0)),
        out_shape=jax.ShapeDtypeStruct((bn, s_len, A_WIDTH), BF16),
        scratch_shapes=[pltpu.VMEM((A_KV_HEADS, s_len, LANES), BF16),
                        pltpu.VMEM((A_KV_HEADS, s_len, LANES), BF16),
                        pltpu.VMEM((s_len, LANES), BF16)],
        compiler_params=_cparams(("parallel", "arbitrary")),
        name="sparse_prompt",
    )(z3, z3, z3, z3, z3, z3)


PAD_ROWS = 8


def _bconv_kernel(gb_ref, gc_ref, xb_ref, w_ref, y_ref, tail_ref, pad_ref):
    tb = gb_ref.shape[0]

    @pl.when(pl.program_id(1) == 0)
    def _():
        pad_ref[0:PAD_ROWS, :] = jnp.zeros((PAD_ROWS, B_WIDTH), F32)

    p = gc_ref[...] * xb_ref[...]
    pad_ref[PAD_ROWS:PAD_ROWS + tb, :] = p
    w = w_ref[...]
    y = (pad_ref[PAD_ROWS - 2:PAD_ROWS - 2 + tb, :] * w[0:1]
         + pad_ref[PAD_ROWS - 1:PAD_ROWS - 1 + tb, :] * w[1:2]
         + p * w[2:3])
    y_ref[...] = (gb_ref[...] * y).astype(y_ref.dtype)
    tail_ref[...] = pad_ref[PAD_ROWS + tb - (B_CONV - 1):PAD_ROWS + tb, :]
    pad_ref[0:PAD_ROWS, :] = pad_ref[tb:tb + PAD_ROWS, :]


def _bconv_prompt(z3, conv_w):
    bn, s_len, _ = z3.shape
    tb = min(256, s_len)
    nt = s_len // tb
    spec = lambda name: pl.BlockSpec((None, tb, 512), lambda b, t: (b, t, _blk(name, 512)))
    return pl.pallas_call(
        _bconv_kernel,
        grid=(bn, nt),
        in_specs=[spec("gb"), spec("gc"), spec("xb"),
                  pl.BlockSpec((B_CONV, B_WIDTH), lambda b, t: (0, 0))],
        out_specs=[pl.BlockSpec((None, tb, B_WIDTH), lambda b, t: (b, t, 0)),
                   pl.BlockSpec((None, B_CONV - 1, B_WIDTH), lambda b, t: (b, 0, 0))],
        out_shape=[jax.ShapeDtypeStruct((bn, s_len, B_WIDTH), BF16),
                   jax.ShapeDtypeStruct((bn, B_CONV - 1, B_WIDTH), F32)],
        scratch_shapes=[pltpu.VMEM((PAD_ROWS + tb, B_WIDTH), F32)],
        compiler_params=_cparams(("parallel", "arbitrary")),
        name="bconv_prompt",
    )(z3, z3, z3, conv_w)


def _lru_coeffs(xc, wr, br, wi, bi, lam):
    xcb = xc.astype(BF16)
    r = jax.nn.sigmoid(_dot(xcb, wr) + br)
    ig = jax.nn.sigmoid(_dot(xcb, wi) + bi)
    nl = -lam
    softplus = jnp.maximum(nl, 0.0) + jnp.log1p(jnp.exp(-jnp.abs(nl)))
    log_a = -LRU_C * r * softplus
    a = jnp.exp(log_a)
    th = jnp.tanh(log_a)
    u = jnp.sqrt(-2.0 * th / (1.0 - th)) * (ig * xc)
    return a, u


def _rglru_kernel(dx_ref, dg_ref, cw_ref, cb_ref, wr_ref, br_ref, wi_ref, bi_ref, lam_ref,
                  y_ref, dtail_ref, hlast_ref, pad_ref, h_ref):
    tb = dx_ref.shape[0]

    @pl.when(pl.program_id(1) == 0)
    def _():
        pad_ref[0:PAD_ROWS, :] = jnp.zeros((PAD_ROWS, D_WIDTH), F32)
        h_ref[...] = jnp.zeros(h_ref.shape, F32)

    x = dx_ref[...]
    pad_ref[PAD_ROWS:PAD_ROWS + tb, :] = x
    cw = cw_ref[...]
    xc = pad_ref[PAD_ROWS - 3:PAD_ROWS - 3 + tb, :] * cw[0:1]
    xc = xc + pad_ref[PAD_ROWS - 2:PAD_ROWS - 2 + tb, :] * cw[1:2]
    xc = xc + pad_ref[PAD_ROWS - 1:PAD_ROWS - 1 + tb, :] * cw[2:3]
    xc = xc + x * cw[3:4]
    xc = xc + cb_ref[...]
    a, u = _lru_coeffs(xc, wr_ref[...], br_ref[...], wi_ref[...], bi_ref[...], lam_ref[...])
    row = lax.broadcasted_iota(jnp.int32, a.shape, 0)
    d = 1
    while d < tb:
        keep = row >= d
        a_prev = jnp.where(keep, pltpu.roll(a, d, 0), 1.0)
        u_prev = jnp.where(keep, pltpu.roll(u, d, 0), 0.0)
        u = a * u_prev + u
        a = a * a_prev
        d *= 2
    h = a * h_ref[0:1, :] + u
    y_ref[...] = (jax.nn.gelu(dg_ref[...]) * h).astype(y_ref.dtype)
    h_ref[0:1, :] = h[tb - 1:tb, :]
    hlast_ref[...] = h[tb - 1:tb, :]
    dtail_ref[...] = pad_ref[PAD_ROWS + tb - (D_CONV - 1):PAD_ROWS + tb, :]
    pad_ref[0:PAD_ROWS, :] = pad_ref[tb:tb + PAD_ROWS, :]


def _rglru_prompt(z3, cw, cb, wr, br, wi, bi, lam):
    bn, s_len, _ = z3.shape
    tb = min(256, s_len)
    nt = s_len // tb
    spec = lambda name: pl.BlockSpec((None, tb, 512), lambda b, t: (b, t, _blk(name, 512)))
    vec = pl.BlockSpec((1, D_WIDTH), lambda b, t: (0, 0))
    mat = pl.BlockSpec((D_WIDTH, D_WIDTH), lambda b, t: (0, 0))
    return pl.pallas_call(
        _rglru_kernel,
        grid=(bn, nt),
        in_specs=[spec("dx"), spec("dg"), pl.BlockSpec((D_CONV, D_WIDTH), lambda b, t: (0, 0)),
                  vec, mat, vec, mat, vec, vec],
        out_specs=[pl.BlockSpec((None, tb, D_WIDTH), lambda b, t: (b, t, 0)),
                   pl.BlockSpec((None, D_CONV - 1, D_WIDTH), lambda b, t: (b, 0, 0)),
                   pl.BlockSpec((None, 1, D_WIDTH), lambda b, t: (b, 0, 0))],
        out_shape=[jax.ShapeDtypeStruct((bn, s_len, D_WIDTH), BF16),
                   jax.ShapeDtypeStruct((bn, D_CONV - 1, D_WIDTH), F32),
                   jax.ShapeDtypeStruct((bn, 1, D_WIDTH), F32)],
        scratch_shapes=[pltpu.VMEM((PAD_ROWS + tb, D_WIDTH), F32),
                        pltpu.VMEM((8, D_WIDTH), F32)],
        compiler_params=_cparams(("parallel", "arbitrary")),
        name="rglru_prompt",
    )(z3, z3, cw, cb, wr, br, wi, bi, lam)


def _sample_bd_kernel(gb_ref, gc_ref, xb_ref, dg_ref, dx_ref, sb_ref, sd_ref, h0_ref,
                      bw_ref, cw_ref, cb_ref, wr_ref, br_ref, wi_ref, bi_ref, lam_ref,
                      yb_ref, yd_ref, bnew_ref, dnew_ref, hnew_ref):
    w = B_WIDTH
    p = gc_ref[...] * xb_ref[...]
    bw = bw_ref[...]
    b0 = sb_ref[:, 0:w]
    b1 = sb_ref[:, w:2 * w]
    y = b0 * bw[0:1] + b1 * bw[1:2] + p * bw[2:3]
    yb_ref[...] = (gb_ref[...] * y).astype(yb_ref.dtype)
    bnew_ref[:, 0:w] = b1
    bnew_ref[:, w:2 * w] = p

    w = D_WIDTH
    x = dx_ref[...]
    cw = cw_ref[...]
    d0 = sd_ref[:, 0:w]
    d1 = sd_ref[:, w:2 * w]
    d2 = sd_ref[:, 2 * w:3 * w]
    xc = d0 * cw[0:1]
    xc = xc + d1 * cw[1:2]
    xc = xc + d2 * cw[2:3]
    xc = xc + x * cw[3:4]
    xc = xc + cb_ref[...]
    a, u = _lru_coeffs(xc, wr_ref[...], br_ref[...], wi_ref[...], bi_ref[...], lam_ref[...])
    h = a * h0_ref[...] + u
    hnew_ref[...] = h
    yd_ref[...] = (jax.nn.gelu(dg_ref[...]) * h).astype(yd_ref.dtype)
    dnew_ref[:, 0:w] = d1
    dnew_ref[:, w:2 * w] = d2
    dnew_ref[:, 2 * w:3 * w] = x


def _sample_bd(z, sb, sd, h0, bw, cw, cb, wr, br, wi, bi, lam):
    m = z.shape[0]
    zspec = lambda name: pl.BlockSpec((m, 512), lambda i: (0, _blk(name, 512)))
    full = lambda a: pl.BlockSpec(a.shape, lambda i: (0,) * a.ndim)
    args = (sb, sd, h0, bw, cw, cb, wr, br, wi, bi, lam)
    out_shapes = [jax.ShapeDtypeStruct((m, B_WIDTH), BF16),
                  jax.ShapeDtypeStruct((m, D_WIDTH), BF16),
                  jax.ShapeDtypeStruct((m, (B_CONV - 1) * B_WIDTH), F32),
                  jax.ShapeDtypeStruct((m, (D_CONV - 1) * D_WIDTH), F32),
                  jax.ShapeDtypeStruct((m, D_WIDTH), F32)]
    return pl.pallas_call(
        _sample_bd_kernel,
        grid=(1,),
        in_specs=[zspec("gb"), zspec("gc"), zspec("xb"), zspec("dg"), zspec("dx")]
                 + [full(a) for a in args],
        out_specs=[pl.BlockSpec(s.shape, lambda i: (0, 0)) for s in out_shapes],
        out_shape=out_shapes,
        compiler_params=_cparams(("arbitrary",)),
        name="sample_bd",
    )(z, z, z, z, z, *args)


def _merge_kernel(ya_ref, yb_ref, yc_ref, yd_ref, g0_ref, g1_ref, g2_ref, g3_ref, x_ref,
                  wa_ref, wb_ref, wc_ref, wd_ref, wo_ref, gp_ref, o_ref):
    acc = jax.nn.sigmoid(g0_ref[...]) * _dot(ya_ref[...], wa_ref[...])
    acc = acc + jax.nn.sigmoid(g1_ref[...]) * _dot(yb_ref[...], wb_ref[...])
    acc = acc + jax.nn.sigmoid(g2_ref[...]) * _dot(yc_ref[...], wc_ref[...])
    acc = acc + jax.nn.sigmoid(g3_ref[...]) * _dot(yd_ref[...], wd_ref[...])
    m = _dot(acc.astype(BF16), wo_ref[...])
    o_ref[...] = x_ref[...] + _rms(m, gp_ref[...])


def _merge(ya, yb, yc, yd, z, x, wa, wb, wc, wd, wo, gp, tm):
    m = x.shape[0]
    tm = min(tm, m)
    row = lambda w: pl.BlockSpec((tm, w), lambda i: (i, 0))
    gate = lambda k: pl.BlockSpec((tm, D_MODEL), lambda i: (i, _blk("gt", D_MODEL) + k))
    const = lambda a: pl.BlockSpec(a.shape, lambda i: (0, 0))
    return pl.pallas_call(
        _merge_kernel,
        grid=(m // tm,),
        in_specs=[row(512), row(512), row(512), row(512), gate(0), gate(1), gate(2), gate(3),
                  row(D_MODEL), const(wa), const(wb), const(wc), const(wd), const(wo), const(gp)],
        out_specs=row(D_MODEL),
        out_shape=jax.ShapeDtypeStruct((m, D_MODEL), F32),
        compiler_params=_cparams(("parallel",)),
        name="merge",
    )(ya, yb, yc, yd, z, z, z, z, x, wa, wb, wc, wd, wo, gp)


def _ffn_kernel(x_ref, gpre_ref, wg_ref, wu_ref, wd_ref, gpost_ref, o_ref, h_ref, acc_ref, *, nf):
    j = pl.program_id(1)

    @pl.when(j == 0)
    def _():
        h_ref[...] = _rms(x_ref[...], gpre_ref[...]).astype(BF16)
        acc_ref[...] = jnp.zeros(acc_ref.shape, F32)

    h = h_ref[...]
    act = jax.nn.silu(_dot(h, wg_ref[...])) * _dot(h, wu_ref[...])
    acc_ref[...] += _dot(act.astype(BF16), wd_ref[...])

    @pl.when(j == nf - 1)
    def _():
        o_ref[...] = x_ref[...] + _rms(acc_ref[...], gpost_ref[...])


def _ffn(x, gpre, wg, wu, wd, gpost, tm, tf=256):
    m = x.shape[0]
    tm = min(tm, m)
    nf = D_FF // tf
    return pl.pallas_call(
        functools.partial(_ffn_kernel, nf=nf),
        grid=(m // tm, nf),
        in_specs=[pl.BlockSpec((tm, D_MODEL), lambda i, j: (i, 0)),
                  pl.BlockSpec((1, D_MODEL), lambda i, j: (0, 0)),
                  pl.BlockSpec((D_MODEL, tf), lambda i, j: (0, j)),
                  pl.BlockSpec((D_MODEL, tf), lambda i, j: (0, j)),
                  pl.BlockSpec((tf, D_MODEL), lambda i, j: (j, 0)),
                  pl.BlockSpec((1, D_MODEL), lambda i, j: (0, 0))],
        out_specs=pl.BlockSpec((tm, D_MODEL), lambda i, j: (i, 0)),
        out_shape=jax.ShapeDtypeStruct((m, D_MODEL), F32),
        scratch_shapes=[pltpu.VMEM((tm, D_MODEL), BF16), pltpu.VMEM((tm, D_MODEL), F32)],
        compiler_params=_cparams(("parallel", "arbitrary")),
        name="ffn",
    )(x, gpre, wg, wu, wd, gpost)


IDX_ROWS = 8
PAGES_PER_SCORE_STEP = 32
PAGES_PER_ATTN_STEP = 32


def _sample_scores_kernel(pt_ref, iq_ref, iw_ref, ikn_ref, *rest, g_pages, n_pages):
    ki_refs = rest[:g_pages]
    o_ref = rest[g_pages]
    j = pl.program_id(1)
    qi = iq_ref[...].astype(BF16)
    wi = iw_ref[...]
    rel = jnp.concatenate([_dot(qi, ki_refs[g][...].astype(BF16)) for g in range(g_pages)], axis=-1)
    sc = jnp.sum(wi * jnp.maximum(rel, 0.0), axis=0, keepdims=True)
    start = pl.multiple_of(j * (g_pages * PAGE_SIZE), g_pages * PAGE_SIZE)
    o_ref[:, pl.ds(start, g_pages * PAGE_SIZE)] = sc

    @pl.when(j == n_pages // g_pages - 1)
    def _():
        kn = ikn_ref[...].astype(BF16).astype(F32)
        rel = jnp.maximum(jnp.sum(qi.astype(F32) * kn, axis=1, keepdims=True), 0.0)
        sc = jnp.sum(wi * rel, axis=0, keepdims=True)
        lane = lax.broadcasted_iota(jnp.int32, (1, PAGE_SIZE), 1)
        o_ref[:, n_pages * PAGE_SIZE:(n_pages + 1) * PAGE_SIZE] = jnp.where(lane == 0, sc, -jnp.inf)


def _sample_scores(pt_flat, iq3, iw3, ikn3, cache_ki, layer, n_pages, g_pages):
    bd = iq3.shape[0]
    ns = (n_pages + 1) * PAGE_SIZE

    def page_map(g):
        return lambda b, j, pt: (layer, pt[b * n_pages + j * g_pages + g], 0, 0)

    grid_spec = pltpu.PrefetchScalarGridSpec(
        num_scalar_prefetch=1,
        grid=(bd, n_pages // g_pages),
        in_specs=[pl.BlockSpec((None, IDX_ROWS, IDX_DIM), lambda b, j, pt: (b, 0, 0)),
                  pl.BlockSpec((None, IDX_ROWS, 1), lambda b, j, pt: (b, 0, 0)),
                  pl.BlockSpec((None, 1, IDX_DIM), lambda b, j, pt: (b, 0, 0))]
                 + [pl.BlockSpec((None, None, IDX_DIM, PAGE_SIZE), page_map(g)) for g in range(g_pages)],
        out_specs=pl.BlockSpec((None, 1, ns), lambda b, j, pt: (b, 0, 0)),
    )
    return pl.pallas_call(
        functools.partial(_sample_scores_kernel, g_pages=g_pages, n_pages=n_pages),
        grid_spec=grid_spec,
        out_shape=jax.ShapeDtypeStruct((bd, 1, ns), F32),
        compiler_params=_cparams(("parallel", "arbitrary")),
        name="sample_scores",
    )(pt_flat, iq3, iw3, ikn3, *([cache_ki] * g_pages))


def _sample_select_kernel(s_ref, o_ref, *, topk, nbits, n_valid):
    xs0 = s_ref[...]
    col = lax.broadcasted_iota(jnp.int32, xs0.shape, 1)
    allowed = col < n_valid
    xs = jnp.where(allowed, xs0 + 0.0, -jnp.inf)
    sel = _topk_select(xs, allowed, col, topk, nbits, N_BISECT_SAMPLE)
    o_ref[...] = jnp.where(sel, 1.0, 0.0)


def _sample_select(scores, topk, n_valid):
    bd, ns = scores.shape
    nbits = max(1, (ns - 1).bit_length())
    return pl.pallas_call(
        functools.partial(_sample_select_kernel, topk=topk, nbits=nbits, n_valid=n_valid),
        grid=(1,),
        in_specs=[pl.BlockSpec((bd, ns), lambda i: (0, 0))],
        out_specs=pl.BlockSpec((bd, ns), lambda i: (0, 0)),
        out_shape=jax.ShapeDtypeStruct((bd, ns), F32),
        compiler_params=_cparams(("arbitrary",)),
        name="sample_select",
    )(scores)


def _online_update(s, weighted_values, m_ref, l_ref, acc_ref):
    m_old = m_ref[...]
    m_new = jnp.maximum(m_old, jnp.max(s, axis=1, keepdims=True))
    alpha = jnp.exp(m_old - m_new)
    p = jnp.exp(s - m_new)
    l_ref[...] = alpha * l_ref[...] + _rowsum(p)
    acc_ref[...] = alpha * acc_ref[...] + weighted_values(p.astype(BF16))
    m_ref[...] = m_new


def _online_update_one(s, v_row, m_ref, l_ref, acc_ref):
    m_old = m_ref[...]
    m_new = jnp.maximum(m_old, s)
    alpha = jnp.exp(m_old - m_new)
    p = jnp.exp(s - m_new)
    l_ref[...] = alpha * l_ref[...] + p
    acc_ref[...] = alpha * acc_ref[...] + p.astype(BF16).astype(F32) * v_row
    m_ref[...] = m_new


def _sample_attn_kernel(pt_ref, mask_ref, qa_ref, ql_ref, qr_ref, kn_ref, vn_ref, cn_ref, rn_ref,
                        k_hbm, v_hbm, c_hbm, r_hbm, oa_ref, oc_ref, kbuf, vbuf, cbuf, rbuf, sems,
                        ma_ref, la_ref, acca_ref, mc_ref, lc_ref, accc_ref, *, layer, g_pages, n_pages):
    b = pl.program_id(0)
    n_groups = n_pages // g_pages
    streams = ((k_hbm, kbuf), (v_hbm, vbuf), (c_hbm, cbuf), (r_hbm, rbuf))

    def slot_of(sample, jg):
        return jg % 2 if n_groups % 2 == 0 else (sample * n_groups + jg) % 2

    def group_copies(sample, jg):
        slot = slot_of(sample, jg)
        copies = []
        for g in range(g_pages):
            page_id = pt_ref[sample * n_pages + jg * g_pages + g]
            for a, (hbm, buf) in enumerate(streams):
                copies.append(pltpu.make_async_copy(hbm.at[layer, page_id], buf.at[slot, g], sems.at[slot, a]))
        return copies

    def start(sample, jg):
        for c in group_copies(sample, jg):
            c.start()

    def wait(sample, jg):
        for c in group_copies(sample, jg):
            c.wait()

    @pl.when(b == 0)
    def _():
        start(b, 0)

    for m_ref, l_ref, acc_ref in ((ma_ref, la_ref, acca_ref), (mc_ref, lc_ref, accc_ref)):
        m_ref[...] = jnp.full(m_ref.shape, NEG_BIG, F32)
        l_ref[...] = jnp.zeros(l_ref.shape, F32)
        acc_ref[...] = jnp.zeros(acc_ref.shape, F32)

    qa = qa_ref[...].astype(BF16)
    ql = ql_ref[...]
    qr = qr_ref[...].astype(BF16)
    span = g_pages * PAGE_SIZE
    page = lambda x, g: x[:, g * PAGE_SIZE:(g + 1) * PAGE_SIZE]

    for jg in range(n_groups):
        if jg + 1 < n_groups:
            start(b, jg + 1)
        else:
            pl.when(b + 1 < pl.num_programs(0))(functools.partial(start, b + 1, 0))
        wait(b, jg)
        slot = slot_of(b, jg)

        member = mask_ref[:, jg * span:(jg + 1) * span] > 0.5
        s = jnp.concatenate([_dot(qa, kbuf[slot, g].astype(BF16)) for g in range(g_pages)], axis=-1)
        s = jnp.where(member, s * A_SCALE, -jnp.inf)
        _online_update(s, lambda p: sum(_dot_nt(page(p, g), vbuf[slot, g].astype(BF16))
                                        for g in range(g_pages)), ma_ref, la_ref, acca_ref)

        cks = [cbuf[slot, g].astype(BF16) for g in range(g_pages)]
        s = jnp.concatenate([_dot_nt(ql, cks[g]) + _dot(qr, rbuf[slot, g].astype(BF16))
                             for g in range(g_pages)], axis=-1) * C_SCALE
        _online_update(s, lambda p: sum(_dot(page(p, g), cks[g]) for g in range(g_pages)),
                       mc_ref, lc_ref, accc_ref)

    rnd = lambda x: x.astype(BF16).astype(F32)
    member = mask_ref[:, n_pages * PAGE_SIZE:n_pages * PAGE_SIZE + 1] > 0.5
    s = jnp.sum(rnd(qa_ref[...]) * rnd(kn_ref[...]), axis=1, keepdims=True) * A_SCALE
    s = jnp.where(member, s, -jnp.inf)
    _online_update_one(s, rnd(vn_ref[...]), ma_ref, la_ref, acca_ref)
    cn = rnd(cn_ref[...])
    s = (jnp.sum(ql.astype(F32) * cn, axis=1, keepdims=True)
         + jnp.sum(rnd(qr_ref[...]) * rnd(rn_ref[...]), axis=1, keepdims=True)) * C_SCALE
    _online_update_one(s, cn, mc_ref, lc_ref, accc_ref)
    o = acca_ref[...] / la_ref[...]
    first_group = lax.broadcasted_iota(jnp.int32, oa_ref.shape, 0) < A_HEADS // A_KV_HEADS
    oa_ref[...] = jnp.where(first_group, o[:, :A_HEAD_DIM], o[:, A_HEAD_DIM:])
    oc_ref[...] = accc_ref[...] / lc_ref[...]


def _sample_attn(pt_flat, mask3, qa, ql, qr, kn, vn, cn, rn, cache_k, cache_v, cache_c, cache_r,
                 layer, n_pages, g_pages):
    bd = qa.shape[0]
    ns = mask3.shape[-1]

    kv_rows = A_KV_HEADS * A_HEAD_DIM
    per_b = lambda r, w: pl.BlockSpec((None, r, w), lambda b, pt: (b, 0, 0))
    in_hbm = pl.BlockSpec(memory_space=pl.ANY)
    slots = lambda r, w: pltpu.VMEM((2, g_pages, r, w), F32)
    grid_spec = pltpu.PrefetchScalarGridSpec(
        num_scalar_prefetch=1,
        grid=(bd,),
        in_specs=[per_b(1, ns), per_b(A_HEADS, LANES), per_b(C_HEADS, C_KV_RANK), per_b(C_HEADS, C_ROPE),
                  per_b(1, LANES), per_b(1, LANES), per_b(1, C_KV_RANK), per_b(1, C_ROPE),
                  in_hbm, in_hbm, in_hbm, in_hbm],
        out_specs=[per_b(A_HEADS, A_HEAD_DIM), per_b(C_HEADS, C_KV_RANK)],
        scratch_shapes=[slots(kv_rows, PAGE_SIZE), slots(kv_rows, PAGE_SIZE),
                        slots(PAGE_SIZE, C_KV_RANK), slots(C_ROPE, PAGE_SIZE),
                        pltpu.SemaphoreType.DMA((2, 4)),
                        pltpu.VMEM((A_HEADS, 1), F32), pltpu.VMEM((A_HEADS, 1), F32),
                        pltpu.VMEM((A_HEADS, LANES), F32),
                        pltpu.VMEM((C_HEADS, 1), F32), pltpu.VMEM((C_HEADS, 1), F32),
                        pltpu.VMEM((C_HEADS, C_KV_RANK), F32)],
    )
    return pl.pallas_call(
        functools.partial(_sample_attn_kernel, layer=layer, g_pages=g_pages, n_pages=n_pages),
        grid_spec=grid_spec,
        out_shape=[jax.ShapeDtypeStruct((bd, A_HEADS, A_HEAD_DIM), F32),
                   jax.ShapeDtypeStruct((bd, C_HEADS, C_KV_RANK), F32)],
        compiler_params=_cparams(("arbitrary",)),
        name="sample_attn",
    )(pt_flat, mask3, qa, ql, qr, kn, vn, cn, rn, cache_k, cache_v, cache_c, cache_r)


def _uv_kernel(o_ref, wuv_ref, y_ref):
    w = 2 * C_KV_RANK
    outs = [_dot(o_ref[:, w * p:w * (p + 1)].astype(BF16), wuv_ref[p]) for p in range(C_HEADS // 2)]
    y_ref[...] = jnp.concatenate(outs, axis=-1).astype(y_ref.dtype)


def _uv_proj(o_lat, wuvp):
    m = o_lat.shape[0]
    return pl.pallas_call(
        _uv_kernel,
        grid=(1,),
        in_specs=[pl.BlockSpec(o_lat.shape, lambda i: (0, 0)),
                  pl.BlockSpec(wuvp.shape, lambda i: (0, 0, 0))],
        out_specs=pl.BlockSpec((m, C_WIDTH), lambda i: (0, 0)),
        out_shape=jax.ShapeDtypeStruct((m, C_WIDTH), BF16),
        compiler_params=_cparams(("arbitrary",)),
        name="uv_proj",
    )(o_lat, wuvp)


def _rope_tables(pos):
    half = C_ROPE // 2
    freqs = ROPE_THETA ** (-jnp.arange(half, dtype=F32) / half)
    ang = pos.astype(F32)[:, None] * freqs[None, :]
    return jnp.tile(jnp.cos(ang), (1, C_HEADS)), jnp.tile(jnp.sin(ang), (1, C_HEADS))


def _layer_weights(l, w_in, w_out_a, conv_b, w_out_b, g_cq, w_uq, g_ckv, w_uk, w_uv, w_out_c,
                   conv_d_w, conv_d_b, w_rg, b_rg, w_ig, b_ig, lru_lambda, w_out_d, w_o,
                   w_ffn_gate, w_ffn_up, w_ffn_down):
    lw = {}
    lw["w_in"] = _relayout_w_in(jnp.transpose(w_in, (0, 2, 1))[l])
    uq = w_uq[l].reshape(C_Q_RANK, C_HEADS, C_NOPE + C_ROPE)
    half = C_ROPE // 2
    lw["w_uq"] = jnp.concatenate(
        [uq[:, :, :C_NOPE].reshape(C_Q_RANK, -1),
         uq[:, :, C_NOPE:C_NOPE + half].reshape(C_Q_RANK, -1),
         uq[:, :, C_NOPE + half:].reshape(C_Q_RANK, -1)], axis=1).astype(BF16)
    ukt = jnp.transpose(w_uk[l], (1, 2, 0))
    zk = jnp.zeros((C_NOPE, C_KV_RANK), F32)
    lw["w_uk"] = jnp.stack([
        jnp.concatenate([jnp.concatenate([ukt[2 * p], zk], axis=1),
                         jnp.concatenate([zk, ukt[2 * p + 1]], axis=1)], axis=0)
        for p in range(C_HEADS // 2)]).astype(BF16)
    uv = jnp.transpose(w_uv[l], (1, 0, 2))
    zv = jnp.zeros((C_KV_RANK, C_VDIM), F32)
    lw["w_uv"] = jnp.stack([
        jnp.concatenate([jnp.concatenate([uv[2 * p], zv], axis=1),
                         jnp.concatenate([zv, uv[2 * p + 1]], axis=1)], axis=0)
        for p in range(C_HEADS // 2)]).astype(BF16)
    eye = jnp.eye(D_BLOCKS, dtype=F32)
    bd = lambda w: (eye[:, None, :, None] * w[:, :, None, :]).reshape(D_WIDTH, D_WIDTH).astype(BF16)
    lw["w_rg"] = bd(w_rg[l])
    lw["w_ig"] = bd(w_ig[l])
    row = lambda v: v[l][None, :]
    lw["b_rg"], lw["b_ig"], lw["lam"], lw["conv_d_b"] = row(b_rg), row(b_ig), row(lru_lambda), row(conv_d_b)
    lw["g_cq"], lw["g_ckv"] = row(g_cq), row(g_ckv)
    lw["conv_b"], lw["conv_d_w"] = conv_b[l], conv_d_w[l]
    for name, w in (("w_out_a", w_out_a), ("w_out_b", w_out_b), ("w_out_c", w_out_c),
                    ("w_out_d", w_out_d), ("w_o", w_o), ("w_ffn_gate", w_ffn_gate),
                    ("w_ffn_up", w_ffn_up), ("w_ffn_down", w_ffn_down)):
        lw[name] = w[l].astype(BF16)
    return lw


def _zcols(z, name, lo, hi):
    o = _DST[name][0]
    return z[..., o + lo:o + hi]


def _state_from_z(z):
    hd = A_HEAD_DIM
    ak = jnp.concatenate([_zcols(z, "ak", 0, hd), _zcols(z, "ak", 2 * hd, 3 * hd)], axis=-1)
    av = jnp.concatenate([_zcols(z, "av", 0, hd), _zcols(z, "av", 2 * hd, 3 * hd)], axis=-1)
    ik = _zcols(z, "ik", 0, IDX_DIM)
    return ak, av, ik


def _krope_state(kr):
    half = C_ROPE // 2
    return jnp.concatenate([kr[:, :half], kr[:, LANES:LANES + half]], axis=-1)


def _prompt_layer(x, lw, gains, bn, s_len, cos, sin):
    g_mix_pre, g_mix_post, g_ffn_pre, g_ffn_post = gains
    m = bn * s_len
    z = _inproj(x, g_mix_pre, lw["w_in"], tm=1024)
    z3 = z.reshape(bn, s_len, N_PROJ)
    tm_c = min(512, s_len)
    nb = s_len // tm_c
    q, kcat, ckvn, kr, _ = _cprep(z, cos, sin, lambda i: (i % nb, 0), lw["g_cq"], lw["g_ckv"],
                                  lw["w_uq"], lw["w_uk"], tm=tm_c)
    yc = _mla_prompt(q, kcat, lw["w_uv"], bn, s_len)
    ya = _sparse_prompt(z3, min(TOPK_MAX, s_len // 4)).reshape(m, A_WIDTH)
    yb, b_new = _bconv_prompt(z3, lw["conv_b"])
    yd, d_new, h_new = _rglru_prompt(z3, lw["conv_d_w"], lw["conv_d_b"], lw["w_rg"], lw["b_rg"],
                                     lw["w_ig"], lw["b_ig"], lw["lam"])
    x = _merge(ya, yb.reshape(m, B_WIDTH), yc, yd.reshape(m, D_WIDTH), z, x,
               lw["w_out_a"], lw["w_out_b"], lw["w_out_c"], lw["w_out_d"], lw["w_o"], g_mix_post, tm=512)
    x = _ffn(x, g_ffn_pre, lw["w_ffn_gate"], lw["w_ffn_up"], lw["w_ffn_down"], g_ffn_post, tm=1024)
    ak, av, ik = _state_from_z(z3)
    st = (ak.reshape(bn, s_len, A_KV_HEADS, A_HEAD_DIM), av.reshape(bn, s_len, A_KV_HEADS, A_HEAD_DIM),
          ik, ckvn.reshape(bn, s_len, C_KV_RANK), _krope_state(kr).reshape(bn, s_len, C_ROPE),
          b_new, d_new, h_new.reshape(bn, D_WIDTH))
    return x, st


def _sample_layer(x, lw, gains, layer, caches, states, pt_flat, n_pages, cos, sin, g_pages):
    g_mix_pre, g_mix_post, g_ffn_pre, g_ffn_post = gains
    cache_k, cache_v, cache_ki, cache_c, cache_r = caches
    sb, sd, h0 = states
    bd = x.shape[0]
    z = _inproj(x, g_mix_pre, lw["w_in"], tm=bd)
    q, _, ckvn, kr, qrope = _cprep(z, cos, sin, lambda i: (0, 0), lw["g_cq"], lw["g_ckv"],
                                   lw["w_uq"], lw["w_uk"], tm=bd)
    ak, av, ik = _state_from_z(z)
    krope = _krope_state(kr)

    iq3 = jnp.pad(_zcols(z, "iq", 0, 256).reshape(bd, IDX_HEADS, IDX_DIM),
                  ((0, 0), (0, IDX_ROWS - IDX_HEADS), (0, 0)))
    iw3 = jnp.pad(_zcols(z, "iw", 0, IDX_HEADS), ((0, 0), (0, IDX_ROWS - IDX_HEADS)))[:, :, None]
    scores = _sample_scores(pt_flat, iq3, iw3, ik[:, None, :], cache_ki, layer, n_pages, g_pages[0])
    total = n_pages * PAGE_SIZE + 1
    mask = _sample_select(scores.reshape(bd, -1), min(TOPK_MAX, total // 4), total)

    aq = _zcols(z, "aq", 0, A_WIDTH).reshape(bd, A_KV_HEADS, A_HEADS // A_KV_HEADS, A_HEAD_DIM)
    zq = jnp.zeros_like(aq[:, 0])
    qa = jnp.concatenate([jnp.concatenate([aq[:, 0], zq], axis=-1),
                          jnp.concatenate([zq, aq[:, 1]], axis=-1)], axis=1)
    ql = jnp.transpose(q[:, :, :C_KV_RANK], (1, 0, 2))
    half = C_ROPE // 2
    qr = jnp.concatenate([qrope[:, :LANES].reshape(bd, C_HEADS, half),
                          qrope[:, LANES:].reshape(bd, C_HEADS, half)], axis=-1)
    oa, oc = _sample_attn(pt_flat, mask[:, None, :], qa, ql, qr, ak[:, None, :], av[:, None, :],
                          ckvn[:, None, :], krope[:, None, :], cache_k, cache_v, cache_c, cache_r,
                          layer, n_pages, g_pages[1])
    ya = oa.reshape(bd, A_WIDTH).astype(BF16)
    yc = _uv_proj(oc.reshape(bd, C_HEADS * C_KV_RANK), lw["w_uv"])

    yb, yd, b_new, d_new, h_new = _sample_bd(
        z, sb.reshape(bd, -1), sd.reshape(bd, -1), h0, lw["conv_b"], lw["conv_d_w"], lw["conv_d_b"],
        lw["w_rg"], lw["b_rg"], lw["w_ig"], lw["b_ig"], lw["lam"])
    x = _merge(ya, yb, yc, yd, z, x, lw["w_out_a"], lw["w_out_b"], lw["w_out_c"], lw["w_out_d"],
               lw["w_o"], g_mix_post, tm=bd)
    x = _ffn(x, g_ffn_pre, lw["w_ffn_gate"], lw["w_ffn_up"], lw["w_ffn_down"], g_ffn_post, tm=bd)
    st = (ak.reshape(bd, 1, A_KV_HEADS, A_HEAD_DIM), av.reshape(bd, 1, A_KV_HEADS, A_HEAD_DIM),
          ik[:, None, :], ckvn[:, None, :], krope[:, None, :],
          b_new.reshape(bd, B_CONV - 1, B_WIDTH), d_new.reshape(bd, D_CONV - 1, D_WIDTH), h_new)
    return x, st


def kernel(x_prompt, x_sample, cache_a_k, cache_a_v, cache_a_kidx, cache_c_kv, cache_c_krope, state_b_conv, state_d_conv, state_d_h, page_table, g_mix_pre, g_mix_post, g_ffn_pre, g_ffn_post, w_in, w_out_a, conv_b, w_out_b, g_cq, w_uq, g_ckv, w_uk, w_uv, w_out_c, conv_d_w, conv_d_b, w_rg, b_rg, w_ig, b_ig, lru_lambda, w_out_d, w_o, w_ffn_gate, w_ffn_up, w_ffn_down):
    bn, s_len, _ = x_prompt.shape
    bd, t_len, _ = x_sample.shape
    assert t_len == 1 and s_len % Q_BLOCK == 0
    depth = w_in.shape[0]
    n_pages = page_table.shape[1]
    n_pool = cache_a_k.shape[1]
    past = n_pages * PAGE_SIZE
    largest_group = lambda cap: max(g for g in range(1, cap + 1) if n_pages % g == 0)
    g_pages = (largest_group(PAGES_PER_SCORE_STEP), largest_group(PAGES_PER_ATTN_STEP))

    cos_p, sin_p = _rope_tables(jnp.arange(s_len))
    cos_s, sin_s = _rope_tables(jnp.full((bd,), past))
    pt_flat = page_table.reshape(-1)
    kv_t = lambda c: jnp.transpose(c, (0, 1, 3, 4, 2)).reshape(
        depth, n_pool, A_KV_HEADS * A_HEAD_DIM, PAGE_SIZE)
    caches = (kv_t(cache_a_k), kv_t(cache_a_v), jnp.transpose(cache_a_kidx, (0, 1, 3, 2)),
              cache_c_kv, jnp.transpose(cache_c_krope, (0, 1, 3, 2)))

    xp = x_prompt.reshape(bn * s_len, D_MODEL)
    xs = x_sample.reshape(bd, D_MODEL)
    p_states, s_states = [], []
    for l in range(depth):
        lw = _layer_weights(l, w_in, w_out_a, conv_b, w_out_b, g_cq, w_uq, g_ckv, w_uk, w_uv, w_out_c,
                            conv_d_w, conv_d_b, w_rg, b_rg, w_ig, b_ig, lru_lambda, w_out_d, w_o,
                            w_ffn_gate, w_ffn_up, w_ffn_down)
        gains = tuple(g[l][None, :] for g in (g_mix_pre, g_mix_post, g_ffn_pre, g_ffn_post))
        xs, st_s = _sample_layer(xs, lw, gains, l, caches,
                                 (state_b_conv[l], state_d_conv[l], state_d_h[l]),
                                 pt_flat, n_pages, cos_s, sin_s, g_pages)
        xp, st_p = _prompt_layer(xp, lw, gains, bn, s_len, cos_p, sin_p)
        p_states.append(st_p)
        s_states.append(st_s)
    p_out = [jnp.stack([st[j] for st in p_states]) for j in range(8)]
    s_out = [jnp.stack([st[j] for st in s_states]) for j in range(8)]
    return (xp.reshape(bn, s_len, D_MODEL), xs.reshape(bd, 1, D_MODEL), *p_out, *s_out)
```
